```python
import jax
import jax.numpy as jnp
from jax import lax
import numpy as np

D_MODEL = 1024
BATCH = 16
SEQ = 2048
DEPTH = 4

CTX_LEN = 256
GRID_W = 64
N_HEADS = 16
HEAD_DIM = D_MODEL // N_HEADS
D_FF = 2816
N_MIXERS = 2
NA_KH_MAX = 8
NA_KW = 16
NA_QC = 16
NA_KBW = 2 * NA_KW
CONV_W = 3
ROPE_BASE = 10000.0
N_MOD = 9
N_NA_LAYERS = (DEPTH + N_MIXERS - 1) // N_MIXERS
N_CONV_LAYERS = DEPTH // N_MIXERS
DN_ALPHA = (2.0 * DEPTH) ** 0.25
DN_BETA = (8.0 * DEPTH) ** -0.25
LN_EPS = 1e-6

kernel_name = "hybrid_natten_shortconv_deepnorm_dit"


def layer_norm(x, g, b):
    xf = x.astype(jnp.float32)
    mu = jnp.mean(xf, axis=-1, keepdims=True)
    var = jnp.mean(jnp.square(xf - mu), axis=-1, keepdims=True)
    return ((xf - mu) * lax.rsqrt(var + LN_EPS)).astype(x.dtype) * g + b


def modulate(x, shift, scale):
    return x * (1 + scale) + shift


def post_norm_update(x, y, gate, res_w, g, b):
    return layer_norm(DN_ALPHA * x + res_w * gate * y, g, b)


def swiglu_ffn(h, w_in, w_out):
    a, u = jnp.split(h @ w_in, 2, axis=-1)
    return (jax.nn.silu(a) * u) @ w_out


def half_step_ffn(x, shift, scale, gate, w_in, w_out, g, b):
    y = swiglu_ffn(modulate(x, shift, scale), w_in, w_out)
    return post_norm_update(x, y, gate, 0.5, g, b)


def axial_rope_tables(n_tok):
    t = np.arange(n_tok)
    n_freq = HEAD_DIM // 4
    inv_freq = ROPE_BASE ** (-np.arange(n_freq) / n_freq)
    ang_row = (t // GRID_W)[:, None] * inv_freq[None]
    ang_col = (t % GRID_W)[:, None] * inv_freq[None]
    return (jnp.asarray(np.cos(ang_row), jnp.float32), jnp.asarray(np.sin(ang_row), jnp.float32),
            jnp.asarray(np.cos(ang_col), jnp.float32), jnp.asarray(np.sin(ang_col), jnp.float32))


def _rotate(x, cos, sin):
    x1, x2 = jnp.split(x, 2, axis=-1)
    cos = cos[None, :, None, :].astype(x.dtype)
    sin = sin[None, :, None, :].astype(x.dtype)
    return jnp.concatenate([x1 * cos - x2 * sin, x2 * cos + x1 * sin], axis=-1)


def apply_axial_rope(x, tables):
    cos_r, sin_r, cos_c, sin_c = tables
    half = HEAD_DIM // 2
    return jnp.concatenate([_rotate(x[..., :half], cos_r, sin_r),
                            _rotate(x[..., half:], cos_c, sin_c)], axis=-1)


def neighbourhood_attention(hx, hc, w_qkv, w_out, rpb, rope, with_ctx_queries):
    B, S, D = hx.shape
    rows = S // GRID_W
    kh = min(NA_KH_MAX, rows)
    ncb = GRID_W // NA_QC
    nk = kh * NA_KBW
    scale = HEAD_DIM ** -0.5

    qkv = (hx @ w_qkv).reshape(B, S, 3, N_HEADS, HEAD_DIM)
    q = apply_axial_rope(qkv[:, :, 0], rope)
    k = apply_axial_rope(qkv[:, :, 1], rope)
    v = qkv[:, :, 2]
    qkv_c = (hc @ w_qkv).reshape(B, hc.shape[1], 3, N_HEADS, HEAD_DIM)
    q_c, k_c, v_c = qkv_c[:, :, 0], qkv_c[:, :, 1], qkv_c[:, :, 2]

    q_col = np.arange(ncb)[:, None] * NA_QC + np.arange(NA_QC)[None]
    win_c0 = np.clip(q_col - NA_KW // 2, 0, GRID_W - NA_KW)
    blk_c0 = np.clip(np.arange(ncb) * NA_QC - NA_KW // 2, 0, GRID_W - NA_KBW)
    key_col = blk_c0[:, None] + np.arange(NA_KBW)[None]
    col_ok = (key_col[:, None, :] >= win_c0[..., None]) & (key_col[:, None, :] < win_c0[..., None] + NA_KW)
    mask = jnp.asarray(np.broadcast_to(col_ok[:, :, None, :], (ncb, NA_QC, kh, NA_KBW)).reshape(ncb, NA_QC, nk))
    dx_idx = np.clip(key_col[:, None, :] - q_col[..., None] + NA_KW - 1, 0, 2 * NA_KW - 2)
    dx_b = np.broadcast_to(dx_idx[:, :, None, :], (ncb, NA_QC, kh, NA_KBW))

    k_grid = k.reshape(B, rows, GRID_W, N_HEADS, HEAD_DIM)
    v_grid = v.reshape(B, rows, GRID_W, N_HEADS, HEAD_DIM)
    q_rows = jnp.moveaxis(q.reshape(B, rows, ncb, NA_QC, N_HEADS, HEAD_DIM), 1, 0)

    def row_block(args):
        r, q_r = args
        rs = jnp.clip(r - kh // 2, 0, rows - kh)
        k_rows = lax.dynamic_slice_in_dim(k_grid, rs, kh, axis=1)
        v_rows = lax.dynamic_slice_in_dim(v_grid, rs, kh, axis=1)
        k_blk = jnp.moveaxis(k_rows[:, :, key_col], 2, 1).reshape(B, ncb, nk, N_HEADS, HEAD_DIM)
        v_blk = jnp.moveaxis(v_rows[:, :, key_col], 2, 1).reshape(B, ncb, nk, N_HEADS, HEAD_DIM)
        dy_idx = rs + jnp.arange(kh) - r + NA_KH_MAX - 1
        bias = rpb[:, dy_idx[None, None, :, None], dx_b].reshape(N_HEADS, ncb, NA_QC, nk)
        s_win = jnp.einsum('bjqhd,bjkhd->bhjqk', q_r, k_blk).astype(jnp.float32) * scale
        s_win = jnp.where(mask, s_win + bias.astype(jnp.float32)[None], -jnp.inf)
        s_ctx = jnp.einsum('bjqhd,bchd->bhjqc', q_r, k_c).astype(jnp.float32) * scale
        p = jax.nn.softmax(jnp.concatenate([s_win, s_ctx], axis=-1), axis=-1).astype(v.dtype)
        return (jnp.einsum('bhjqk,bjkhd->bjqhd', p[..., :nk], v_blk)
                + jnp.einsum('bhjqc,bchd->bjqhd', p[..., nk:], v_c))

    o = lax.map(row_block, (jnp.arange(rows), q_rows))
    y_lat = jnp.moveaxis(o, 0, 1).reshape(B, S, D) @ w_out

    y_ctx = None
    if with_ctx_queries:
        s = jnp.einsum('bqhd,bkhd->bhqk', q_c, k_c).astype(jnp.float32) * scale
        p = jax.nn.softmax(s, axis=-1).astype(v_c.dtype)
        y_ctx = jnp.einsum('bhqk,bkhd->bqhd', p, v_c).reshape(B, hc.shape[1], D) @ w_out
    return y_lat, y_ctx


def short_conv_mixer(h, w_in, conv_w, w_out):
    b_gate, c_gate, hv = jnp.split(h @ w_in, 3, axis=-1)
    u = c_gate * hv
    n = u.shape[1]
    pad = CONV_W // 2
    up = jnp.pad(u, ((0, 0), (pad, pad), (0, 0)))
    y = conv_w[0] * up[:, 0:n]
    for j in range(1, CONV_W):
        y = y + conv_w[j] * up[:, j:j + n]
    return (b_gate * y) @ w_out


def setup_inputs(seed: int = 0) -> dict:
    key = jax.random.key(seed)
    ks = jax.random.split(key, 16)
    D, F = D_MODEL, D_FF

    def nrm(k, shape, s):
        return jax.random.normal(k, shape, jnp.float32) * s

    return {
        "x": nrm(ks[0], (BATCH, SEQ, D), 1.0),
        "c": nrm(ks[1], (BATCH, D), 1.0),
        "ctx": nrm(ks[2], (BATCH, CTX_LEN, D), 1.0),
        "c_ctx": nrm(ks[3], (D,), 1.0),
        "ada_w": nrm(ks[4], (DEPTH, D, N_MOD * D), 0.5 * D ** -0.5),
        "ada_b": nrm(ks[5], (DEPTH, N_MOD * D), 0.02),
        "ln_g": 1.0 + nrm(ks[6], (DEPTH, 3, D), 0.02),
        "ln_b": nrm(ks[7], (DEPTH, 3, D), 0.02),
        "ffn_w_in": nrm(ks[8], (DEPTH, 2, D, 2 * F), D ** -0.5),
        "ffn_w_out": nrm(ks[9], (DEPTH, 2, F, D), DN_BETA * F ** -0.5),
        "na_w_qkv": nrm(ks[10], (N_NA_LAYERS, D, 3 * D), D ** -0.5),
        "na_w_out": nrm(ks[11], (N_NA_LAYERS, D, D), DN_BETA * D ** -0.5),
        "na_rpb": nrm(ks[12], (N_NA_LAYERS, N_HEADS, 2 * NA_KH_MAX - 1, 2 * NA_KW - 1), 0.1),
        "sc_w_in": nrm(ks[13], (N_CONV_LAYERS, D, 3 * D), D ** -0.5),
        "sc_conv": nrm(ks[14], (N_CONV_LAYERS, CONV_W, D), CONV_W ** -0.5),
        "sc_w_out": nrm(ks[15], (N_CONV_LAYERS, D, D), DN_BETA * D ** -0.5),
    }


def reference(x, c, ctx, c_ctx, ada_w, ada_b, ln_g, ln_b, ffn_w_in, ffn_w_out,
              na_w_qkv, na_w_out, na_rpb, sc_w_in, sc_conv, sc_w_out):
    B, S, D = x.shape
    rope = axial_rope_tables(S)
    last_na = max(range(0, DEPTH, N_MIXERS))
    s_c = jax.nn.silu(c)
    s_cc = jax.nn.silu(c_ctx)
    for i in range(DEPTH):
        ctx_in = i <= last_na
        ctx_out = i < last_na
        j = i // N_MIXERS
        m = (s_c @ ada_w[i] + ada_b[i]).reshape(B, N_MOD, 1, D)
        mc = (s_cc @ ada_w[i] + ada_b[i]).reshape(N_MOD, D)

        x = half_step_ffn(x, m[:, 0], m[:, 1], m[:, 2], ffn_w_in[i, 0], ffn_w_out[i, 0], ln_g[i, 0], ln_b[i, 0])
        if ctx_in:
            ctx = half_step_ffn(ctx, mc[0], mc[1], mc[2], ffn_w_in[i, 0], ffn_w_out[i, 0], ln_g[i, 0], ln_b[i, 0])

        hx = modulate(x, m[:, 3], m[:, 4])
        y_ctx = None
        if i % N_MIXERS == 0:
            hc = modulate(ctx, mc[3], mc[4])
            y_lat, y_ctx = neighbourhood_attention(hx, hc, na_w_qkv[j], na_w_out[j], na_rpb[j], rope, ctx_out)
        else:
            y_lat = short_conv_mixer(hx, sc_w_in[j], sc_conv[j], sc_w_out[j])
            if ctx_out:
                y_ctx = short_conv_mixer(modulate(ctx, mc[3], mc[4]), sc_w_in[j], sc_conv[j], sc_w_out[j])
        x = post_norm_update(x, y_lat, m[:, 5], 1.0, ln_g[i, 1], ln_b[i, 1])
        if ctx_out:
            ctx = post_norm_update(ctx, y_ctx, mc[5], 1.0, ln_g[i, 1], ln_b[i, 1])

        x = half_step_ffn(x, m[:, 6], m[:, 7], m[:, 8], ffn_w_in[i, 1], ffn_w_out[i, 1], ln_g[i, 2], ln_b[i, 2])
        if ctx_out:
            ctx = half_step_ffn(ctx, mc[6], mc[7], mc[8], ffn_w_in[i, 1], ffn_w_out[i, 1], ln_g[i, 2], ln_b[i, 2])
    return x
```

```python
import functools

import numpy as np
import jax
import jax.numpy as jnp
from jax import lax
from jax.experimental import pallas as pl
from jax.experimental.pallas import tpu as pltpu

GRID_W = 64
N_HEADS = 16
N_MIXERS = 2
NA_KH = 8
NA_KW = 16
CONV_W = 3
ROPE_BASE = 10000.0
N_MOD = 9
LN_EPS = 1e-6
MASKED = -1e30

LANES = 128
SUBLANES = 8
HEADS_PER_SLAB = 2
VMEM_LIMIT = 56 * 1024 * 1024

TM_LATENT = 512
F_CHUNK = 256


def _cparams(n_axes):
    return pltpu.CompilerParams(dimension_semantics=("arbitrary",) * n_axes,
                                vmem_limit_bytes=VMEM_LIMIT)


def _resident(block_shape, index_map):
    return pl.BlockSpec(block_shape, index_map, pipeline_mode=pl.Buffered(1))


def _layer_norm(z, g, b):
    mu = jnp.mean(z, axis=-1, keepdims=True)
    zc = z - mu
    var = jnp.mean(zc * zc, axis=-1, keepdims=True)
    return zc * lax.rsqrt(var + LN_EPS) * g + b


def _silu(a):
    return a * jax.nn.sigmoid(a)


def _ada_kernel(c_ref, w_ref, b_ref, o_ref):
    s = _silu(c_ref[...]).astype(jnp.bfloat16)
    o_ref[...] = jnp.dot(s, w_ref[...].astype(jnp.bfloat16),
                         preferred_element_type=jnp.float32) + b_ref[...]


def _ada_modulation(cc, ada_w, ada_b):
    depth, d, n = ada_w.shape
    r = cc.shape[0]
    tn = d
    out = pl.pallas_call(
        _ada_kernel,
        grid=(depth, n // tn),
        in_specs=[pl.BlockSpec((r, d), lambda i, j: (0, 0)),
                  pl.BlockSpec((None, d, tn), lambda i, j: (i, 0, j)),
                  pl.BlockSpec((None, 1, tn), lambda i, j: (i, 0, j))],
        out_specs=pl.BlockSpec((None, r, tn), lambda i, j: (i, 0, j)),
        out_shape=jax.ShapeDtypeStruct((depth, r, n), jnp.float32),
        compiler_params=_cparams(2),
        name="ada_modulation",
    )(cc, ada_w, ada_b.reshape(depth, 1, n))
    return out.reshape(depth, r, N_MOD, d)


def _tok_spec(tm, d):
    return pl.BlockSpec((None, tm, d), lambda b, t: (b, t, 0))


def _mod_spec(layer, mod_row, d):
    if mod_row is None:
        return pl.BlockSpec((None, None, N_MOD, d), lambda b, t: (layer, b, 0, 0))
    return pl.BlockSpec((None, None, N_MOD, d), lambda b, t: (layer, mod_row, 0, 0))


def _ln_spec(layer, k, d):
    return pl.BlockSpec((None, None, 1, d), lambda b, t: (layer, k, 0, 0))


def _ffn_kernel(x_ref, mod_ref, win_ref, wout_ref, g_ref, b_ref, o_ref, *, k0, alpha, f_chunk):
    x = x_ref[...]
    shift, scale, gate = mod_ref[k0:k0 + 1, :], mod_ref[k0 + 1:k0 + 2, :], mod_ref[k0 + 2:k0 + 3, :]
    h = (x * (1 + scale) + shift).astype(jnp.bfloat16)
    f = wout_ref.shape[0]
    y = jnp.zeros(x.shape, jnp.float32)
    for f0 in range(0, f, f_chunk):
        fc = min(f_chunk, f - f0)
        a = jnp.dot(h, win_ref[:, f0:f0 + fc], preferred_element_type=jnp.float32)
        u = jnp.dot(h, win_ref[:, f + f0:f + f0 + fc], preferred_element_type=jnp.float32)
        hid = (_silu(a) * u).astype(jnp.bfloat16)
        y = y + jnp.dot(hid, wout_ref[f0:f0 + fc, :], preferred_element_type=jnp.float32)
    o_ref[...] = _layer_norm(alpha * x + 0.5 * gate * y, g_ref[...], b_ref[...])


def _half_step_ffn(xt, mod, ln_g, ln_b, w_in, w_out, *, layer, which, mod_row, tm, alpha):
    nb, s, d = xt.shape
    f = w_out.shape[2]
    kern = functools.partial(_ffn_kernel, k0=0 if which == 0 else 6, alpha=alpha, f_chunk=F_CHUNK)
    ln_k = 0 if which == 0 else 2
    return pl.pallas_call(
        kern,
        grid=(nb, s // tm),
        in_specs=[_tok_spec(tm, d), _mod_spec(layer, mod_row, d),
                  _resident((None, None, d, 2 * f), lambda b, t: (layer, which, 0, 0)),
                  _resident((None, None, f, d), lambda b, t: (layer, which, 0, 0)),
                  _ln_spec(layer, ln_k, d), _ln_spec(layer, ln_k, d)],
        out_specs=_tok_spec(tm, d),
        out_shape=jax.ShapeDtypeStruct(xt.shape, jnp.float32),
        compiler_params=_cparams(2),
        name="half_step_ffn",
    )(xt, mod, w_in, w_out, ln_g, ln_b)


def _rope_slab(v, cos, sin, first_half):
    up = pltpu.roll(v, LANES - 16, 1)
    dn = pltpu.roll(v, 16, 1)
    return v * cos + jnp.where(first_half, up, dn) * sin


def _qkv_kernel(x_ref, mod_ref, w_ref, *rest, rope, q_scale):
    if rope:
        cos_ref, sin_ref, q_ref, k_ref, v_ref = rest
    else:
        q_ref, k_ref, v_ref = rest
    x = x_ref[...]
    d = x.shape[1]
    h = (x * (1 + mod_ref[4:5, :]) + mod_ref[3:4, :]).astype(jnp.bfloat16)
    v_ref[...] = jnp.dot(h, w_ref[:, 2 * d:], preferred_element_type=jnp.float32).astype(v_ref.dtype)
    for out_ref, c0, mult in ((q_ref, 0, q_scale), (k_ref, d, None)):
        y = jnp.dot(h, w_ref[:, c0:c0 + d], preferred_element_type=jnp.float32)
        if mult is not None:
            y = y * mult
        if rope:
            cos, sin = cos_ref[...], sin_ref[...]
            lane = lax.broadcasted_iota(jnp.int32, (1, LANES), 1)
            first_half = (lane % 32) < 16
            for s0 in range(0, d, LANES):
                out_ref[:, s0:s0 + LANES] = _rope_slab(y[:, s0:s0 + LANES], cos, sin,
                                                       first_half).astype(out_ref.dtype)
        else:
            out_ref[...] = y.astype(out_ref.dtype)


def _qkv_projection(xt, mod, w_qkv, rope_tabs, *, layer, j, mod_row, tm, q_scale):
    nb, s, d = xt.shape
    rope = rope_tabs is not None
    in_specs = [_tok_spec(tm, d), _mod_spec(layer, mod_row, d),
                _resident((None, d, 3 * d), lambda b, t: (j, 0, 0))]
    args = [xt, mod, w_qkv]
    if rope:
        in_specs += [pl.BlockSpec((tm, LANES), lambda b, t: (t, 0))] * 2
        args += list(rope_tabs)
    out = jax.ShapeDtypeStruct(xt.shape, jnp.bfloat16)
    return pl.pallas_call(
        functools.partial(_qkv_kernel, rope=rope, q_scale=q_scale),
        grid=(nb, s // tm),
        in_specs=in_specs,
        out_specs=[_tok_spec(tm, d)] * 3,
        out_shape=[out] * 3,
        compiler_params=_cparams(2),
        name="qkv_projection",
    )(*args)


def _stack_heads(q, head0):
    zero = jnp.zeros_like(q)
    return jnp.concatenate([jnp.where(head0, q, zero), jnp.where(head0, zero, q)], axis=0)


def _unstack_heads(o, head0):
    n = o.shape[0] // 2
    return jnp.where(head0, o[:n], o[n:])


def _dot_nt(a, b):
    return lax.dot_general(a, b, (((1,), (1,)), ((), ())), preferred_element_type=jnp.float32)


def _na_kernel(q_ref, k_ref, v_ref, kc_ref, vc_ref, bias_ref, *rest, rows, ctx_queries):
    if ctx_queries:
        qc_ref, o_ref, oc_ref = rest
    else:
        (o_ref,) = rest
    head0 = lax.broadcasted_iota(jnp.int32, (1, LANES), 1) < (LANES // HEADS_PER_SLAB)
    kc, vc = kc_ref[...], vc_ref[...]
    nk = NA_KH * GRID_W

    def row_body(r, carry):
        rs = jnp.clip(r - NA_KH // 2, 0, rows - NA_KH)
        q0 = pl.multiple_of(r * GRID_W, GRID_W)
        k0 = pl.multiple_of(rs * GRID_W, GRID_W)
        qs = _stack_heads(q_ref[pl.ds(q0, GRID_W), :], head0)
        s_win = _dot_nt(qs, k_ref[pl.ds(k0, nk), :]) + bias_ref[r - rs]
        s_ctx = _dot_nt(qs, kc)
        m = jnp.maximum(jnp.max(s_win, axis=-1, keepdims=True), jnp.max(s_ctx, axis=-1, keepdims=True))
        p_win, p_ctx = jnp.exp(s_win - m), jnp.exp(s_ctx - m)
        den = jnp.sum(p_win, axis=-1, keepdims=True) + jnp.sum(p_ctx, axis=-1, keepdims=True)
        o = (jnp.dot(p_win.astype(jnp.bfloat16), v_ref[pl.ds(k0, nk), :], preferred_element_type=jnp.float32)
             + jnp.dot(p_ctx.astype(jnp.bfloat16), vc, preferred_element_type=jnp.float32))
        o_ref[pl.ds(q0, GRID_W), :] = _unstack_heads(o / den, head0).astype(o_ref.dtype)
        return carry

    lax.fori_loop(0, rows, row_body, 0)

    if ctx_queries:
        s = _dot_nt(_stack_heads(qc_ref[...], head0), kc)
        p = jnp.exp(s - jnp.max(s, axis=-1, keepdims=True))
        den = jnp.sum(p, axis=-1, keepdims=True)
        o = jnp.dot(p.astype(jnp.bfloat16), vc, preferred_element_type=jnp.float32)
        oc_ref[...] = _unstack_heads(o / den, head0).astype(oc_ref.dtype)


def _na_bias_table(rpb):
    h = rpb.shape[0]
    off = np.arange(NA_KH)
    dy = off[None, :] - off[:, None] + NA_KH - 1
    col = np.arange(GRID_W)
    win_c0 = np.clip(col - NA_KW // 2, 0, GRID_W - NA_KW)
    allowed = (col[None, :] >= win_c0[:, None]) & (col[None, :] < win_c0[:, None] + NA_KW)
    dx = np.clip(col[None, :] - col[:, None] + NA_KW - 1, 0, 2 * NA_KW - 2)
    tbl = rpb[:, dy[:, None, :, None], dx[None, :, None, :]]
    tbl = jnp.where(jnp.asarray(allowed)[None, None, :, None, :], tbl, MASKED)
    tbl = tbl.reshape(h // HEADS_PER_SLAB, HEADS_PER_SLAB, NA_KH, GRID_W, NA_KH * GRID_W)
    return jnp.moveaxis(tbl, 1, 2).reshape(h // HEADS_PER_SLAB, NA_KH, HEADS_PER_SLAB * GRID_W, NA_KH * GRID_W)


def _neighbourhood_attention(q, k, v, qc, kc, vc, bias, *, ctx_queries):
    nb, s, d = q.shape
    n_ctx = kc.shape[1]
    slabs = d // LANES
    rows = s // GRID_W
    lat = pl.BlockSpec((None, s, LANES), lambda p, b: (b, 0, p))
    ctx = pl.BlockSpec((None, n_ctx, LANES), lambda p, b: (b, 0, p))
    in_specs = [lat, lat, lat, ctx, ctx,
                pl.BlockSpec((None,) + bias.shape[1:], lambda p, b: (p, 0, 0, 0))]
    args = [q, k, v, kc, vc, bias]
    out_specs, out_shape = [lat], [jax.ShapeDtypeStruct(q.shape, jnp.bfloat16)]
    if ctx_queries:
        in_specs.append(ctx)
        args.append(qc)
        out_specs.append(ctx)
        out_shape.append(jax.ShapeDtypeStruct(kc.shape, jnp.bfloat16))
    outs = pl.pallas_call(
        functools.partial(_na_kernel, rows=rows, ctx_queries=ctx_queries),
        grid=(slabs, nb),
        in_specs=in_specs, out_specs=out_specs, out_shape=out_shape,
        compiler_params=_cparams(2),
        name="neighbourhood_attention",
    )(*args)
    return (outs[0], outs[1]) if ctx_queries else (outs[0], None)


def _mix_out_kernel(x_ref, o_ref, mod_ref, w_ref, g_ref, b_ref, out_ref, *, alpha):
    y = jnp.dot(o_ref[...], w_ref[...], preferred_element_type=jnp.float32)
    out_ref[...] = _layer_norm(alpha * x_ref[...] + 1.0 * mod_ref[5:6, :] * y, g_ref[...], b_ref[...])


def _mixer_out(xt, o, mod, ln_g, ln_b, w_out, *, layer, j, mod_row, tm, alpha):
    nb, s, d = xt.shape
    return pl.pallas_call(
        functools.partial(_mix_out_kernel, alpha=alpha),
        grid=(nb, s // tm),
        in_specs=[_tok_spec(tm, d), _tok_spec(tm, d), _mod_spec(layer, mod_row, d),
                  _resident((None, d, d), lambda b, t: (j, 0, 0)),
                  _ln_spec(layer, 1, d), _ln_spec(layer, 1, d)],
        out_specs=_tok_spec(tm, d),
        out_shape=jax.ShapeDtypeStruct(xt.shape, jnp.float32),
        compiler_params=_cparams(2),
        name="mixer_out",
    )(xt, o, mod, w_out, ln_g, ln_b)


def _conv_in_kernel(x_ref, mod_ref, w_ref, bg_ref, u_ref):
    x = x_ref[...]
    d = x.shape[1]
    h = (x * (1 + mod_ref[4:5, :]) + mod_ref[3:4, :]).astype(jnp.bfloat16)
    bg_ref[...] = jnp.dot(h, w_ref[:, :d], preferred_element_type=jnp.float32)
    cg = jnp.dot(h, w_ref[:, d:2 * d], preferred_element_type=jnp.float32)
    hv = jnp.dot(h, w_ref[:, 2 * d:], preferred_element_type=jnp.float32)
    u_ref[...] = cg * hv


def _conv_in(xt, mod, w_in, *, layer, j, mod_row, tm):
    nb, s, d = xt.shape
    out = jax.ShapeDtypeStruct(xt.shape, jnp.float32)
    return pl.pallas_call(
        _conv_in_kernel,
        grid=(nb, s // tm),
        in_specs=[_tok_spec(tm, d), _mod_spec(layer, mod_row, d),
                  _resident((None, d, 3 * d), lambda b, t: (j, 0, 0))],
        out_specs=[_tok_spec(tm, d)] * 2,
        out_shape=[out] * 2,
        compiler_params=_cparams(2),
        name="conv_in",
    )(xt, mod, w_in)


def _conv_out_kernel(x_ref, bg_ref, u_ref, up_ref, un_ref, cw_ref, mod_ref, w_ref, g_ref, b_ref, out_ref, *, alpha):
    t, nt = pl.program_id(1), pl.num_programs(1)
    u = u_ref[...]
    tm = u.shape[0]
    row = lax.broadcasted_iota(jnp.int32, (tm, 1), 0)
    before = jnp.where(t > 0, up_ref[SUBLANES - 1:SUBLANES, :], 0.0)
    after = jnp.where(t < nt - 1, un_ref[0:1, :], 0.0)
    u_prev = jnp.where(row == 0, before, pltpu.roll(u, 1, 0))
    u_next = jnp.where(row == tm - 1, after, pltpu.roll(u, tm - 1, 0))
    y = cw_ref[0:1, :] * u_prev + cw_ref[1:2, :] * u + cw_ref[2:3, :] * u_next
    z = jnp.dot((bg_ref[...] * y).astype(jnp.bfloat16), w_ref[...], preferred_element_type=jnp.float32)
    out_ref[...] = _layer_norm(alpha * x_ref[...] + 1.0 * mod_ref[5:6, :] * z, g_ref[...], b_ref[...])


def _conv_out(xt, bg, u, conv_w, mod, ln_g, ln_b, w_out, *, layer, j, mod_row, tm, alpha):
    nb, s, d = xt.shape
    per_tile = tm // SUBLANES
    last = s // SUBLANES - 1
    prev_spec = pl.BlockSpec((None, SUBLANES, d), lambda b, t: (b, jnp.maximum(t * per_tile - 1, 0), 0))
    next_spec = pl.BlockSpec((None, SUBLANES, d), lambda b, t: (b, jnp.minimum((t + 1) * per_tile, last), 0))
    return pl.pallas_call(
        functools.partial(_conv_out_kernel, alpha=alpha),
        grid=(nb, s // tm),
        in_specs=[_tok_spec(tm, d), _tok_spec(tm, d), _tok_spec(tm, d), prev_spec, next_spec,
                  pl.BlockSpec((None, CONV_W, d), lambda b, t: (j, 0, 0)),
                  _mod_spec(layer, mod_row, d),
                  _resident((None, d, d), lambda b, t: (j, 0, 0)),
                  _ln_spec(layer, 1, d), _ln_spec(layer, 1, d)],
        out_specs=_tok_spec(tm, d),
        out_shape=jax.ShapeDtypeStruct(xt.shape, jnp.float32),
        compiler_params=_cparams(2),
        name="conv_out",
    )(xt, bg, u, u, u, conv_w, mod, w_out, ln_g, ln_b)


def _rope_tables(n_tok, head_dim):
    t = np.arange(n_tok)
    n_freq = head_dim // 4
    inv_freq = ROPE_BASE ** (-np.arange(n_freq) / n_freq)
    ang_row = (t // GRID_W)[:, None] * inv_freq[None]
    ang_col = (t % GRID_W)[:, None] * inv_freq[None]
    cos = np.concatenate([np.cos(ang_row)] * 2 + [np.cos(ang_col)] * 2, axis=1)
    sin = np.concatenate([-np.sin(ang_row), np.sin(ang_row), -np.sin(ang_col), np.sin(ang_col)], axis=1)
    reps = LANES // head_dim
    return (jnp.asarray(np.tile(cos, (1, reps)), jnp.float32),
            jnp.asarray(np.tile(sin, (1, reps)), jnp.float32))


def kernel(x, c, ctx, c_ctx, ada_w, ada_b, ln_g, ln_b, ffn_w_in, ffn_w_out, na_w_qkv, na_w_out, na_rpb,
           sc_w_in, sc_conv, sc_w_out):
    nb, s, d = x.shape
    n_ctx = ctx.shape[1]
    depth = ada_w.shape[0]
    head_dim = d // N_HEADS
    alpha = (2.0 * depth) ** 0.25
    q_scale = head_dim ** -0.5
    last_na = max(range(0, depth, N_MIXERS))
    assert s % TM_LATENT == 0 and TM_LATENT % GRID_W == 0 and s // GRID_W >= NA_KH
    assert LANES // head_dim == HEADS_PER_SLAB

    ctx_row = nb
    n_rows = -(-(nb + 1) // SUBLANES) * SUBLANES
    cc = jnp.concatenate([c, c_ctx[None], jnp.zeros((n_rows - nb - 1, d), c.dtype)], axis=0)
    mod = _ada_modulation(cc, ada_w, ada_b)

    bf = jnp.bfloat16
    ffn_w_in, ffn_w_out = ffn_w_in.astype(bf), ffn_w_out.astype(bf)
    na_w_qkv, na_w_out = na_w_qkv.astype(bf), na_w_out.astype(bf)
    sc_w_in, sc_w_out = sc_w_in.astype(bf), sc_w_out.astype(bf)
    ln_g4, ln_b4 = ln_g.reshape(depth, 3, 1, d), ln_b.reshape(depth, 3, 1, d)
    rope_tabs = _rope_tables(s, head_dim)

    lat = dict(mod_row=None, tm=TM_LATENT)
    cx = dict(mod_row=ctx_row, tm=n_ctx)

    for i in range(depth):
        ctx_in = i <= last_na
        ctx_out = i < last_na
        j = i // N_MIXERS
        ffn = functools.partial(_half_step_ffn, mod=mod, ln_g=ln_g4, ln_b=ln_b4, w_in=ffn_w_in,
                                w_out=ffn_w_out, layer=i, alpha=alpha)

        x = ffn(x, which=0, **lat)
        if ctx_in:
            ctx = ffn(ctx, which=0, **cx)

        if i % N_MIXERS == 0:
            q, k, v = _qkv_projection(x, mod, na_w_qkv, rope_tabs, layer=i, j=j, q_scale=q_scale, **lat)
            qc, kc, vc = _qkv_projection(ctx, mod, na_w_qkv, None, layer=i, j=j, q_scale=q_scale, **cx)
            o, oc = _neighbourhood_attention(q, k, v, qc, kc, vc, _na_bias_table(na_rpb[j]),
                                             ctx_queries=ctx_out)
            mix = functools.partial(_mixer_out, mod=mod, ln_g=ln_g4, ln_b=ln_b4, w_out=na_w_out,
                                    layer=i, j=j, alpha=alpha)
            x = mix(x, o, **lat)
            if ctx_out:
                ctx = mix(ctx, oc, **cx)
        else:
            mix = functools.partial(_conv_out, conv_w=sc_conv, mod=mod, ln_g=ln_g4, ln_b=ln_b4,
                                    w_out=sc_w_out, layer=i, j=j, alpha=alpha)
            bg, u = _conv_in(x, mod, sc_w_in, layer=i, j=j, **lat)
            x = mix(x, bg, u, **lat)
            if ctx_out:
                bg, u = _conv_in(ctx, mod, sc_w_in, layer=i, j=j, **cx)
                ctx = mix(ctx, bg, u, **cx)

        x = ffn(x, which=1, **lat)
        if ctx_out:
            ctx = ffn(ctx, which=1, **cx)
    return x
```

```python
import functools

import numpy as np
import jax
import jax.numpy as jnp
from jax import lax
from jax.experimental import pallas as pl
from jax.experimental.pallas import tpu as pltpu

GRID_W = 64
N_HEADS = 16
N_MIXERS = 2
NA_KH = 8
NA_KW = 16
CONV_W = 3
ROPE_BASE = 10000.0
N_MOD = 9
LN_EPS = 1e-6
MASKED = -1e30

LANES = 128
SUBLANES = 8
HEADS_PER_SLAB = 2
VMEM_LIMIT = 56 * 1024 * 1024

TM_LATENT = 512
F_CHUNK = 256
NA_ROW_UNROLL = 4


def _cparams(n_axes):
    return pltpu.CompilerParams(dimension_semantics=("arbitrary",) * n_axes,
                                vmem_limit_bytes=VMEM_LIMIT)


def _resident(block_shape, index_map):
    return pl.BlockSpec(block_shape, index_map, pipeline_mode=pl.Buffered(1))


def _layer_norm(z, g, b):
    mu = jnp.mean(z, axis=-1, keepdims=True)
    zc = z - mu
    var = jnp.mean(zc * zc, axis=-1, keepdims=True)
    return zc * lax.rsqrt(var + LN_EPS) * g + b


def _silu(a):
    return a * jax.nn.sigmoid(a)


def _ada_kernel(c_ref, w_ref, b_ref, o_ref):
    s = _silu(c_ref[...]).astype(jnp.bfloat16)
    o_ref[...] = jnp.dot(s, w_ref[...].astype(jnp.bfloat16),
                         preferred_element_type=jnp.float32) + b_ref[...]


def _ada_modulation(cc, ada_w, ada_b):
    depth, d, n = ada_w.shape
    r = cc.shape[0]
    tn = d
    out = pl.pallas_call(
        _ada_kernel,
        grid=(depth, n // tn),
        in_specs=[pl.BlockSpec((r, d), lambda i, j: (0, 0)),
                  pl.BlockSpec((None, d, tn), lambda i, j: (i, 0, j)),
                  pl.BlockSpec((None, 1, tn), lambda i, j: (i, 0, j))],
        out_specs=pl.BlockSpec((None, r, tn), lambda i, j: (i, 0, j)),
        out_shape=jax.ShapeDtypeStruct((depth, r, n), jnp.float32),
        compiler_params=_cparams(2),
        name="ada_modulation",
    )(cc, ada_w, ada_b.reshape(depth, 1, n))
    return out.reshape(depth, r, N_MOD, d)


def _tok_spec(tm, d):
    return pl.BlockSpec((None, tm, d), lambda b, t: (b, t, 0))


def _mod_spec(layer, mod_row, d):
    if mod_row is None:
        return pl.BlockSpec((None, None, N_MOD, d), lambda b, t: (layer, b, 0, 0))
    return pl.BlockSpec((None, None, N_MOD, d), lambda b, t: (layer, mod_row, 0, 0))


def _ln_spec(layer, k, d):
    return pl.BlockSpec((None, None, 1, d), lambda b, t: (layer, k, 0, 0))


def _ffn_kernel(x_ref, mod_ref, win_ref, wout_ref, g_ref, b_ref, o_ref, *, k0, alpha, f_chunk):
    x = x_ref[...]
    shift, scale, gate = mod_ref[k0:k0 + 1, :], mod_ref[k0 + 1:k0 + 2, :], mod_ref[k0 + 2:k0 + 3, :]
    h = (x * (1 + scale) + shift).astype(jnp.bfloat16)
    f = wout_ref.shape[0]
    y = jnp.zeros(x.shape, jnp.float32)
    for f0 in range(0, f, f_chunk):
        fc = min(f_chunk, f - f0)
        a = jnp.dot(h, win_ref[:, f0:f0 + fc], preferred_element_type=jnp.float32)
        u = jnp.dot(h, win_ref[:, f + f0:f + f0 + fc], preferred_element_type=jnp.float32)
        hid = (_silu(a) * u).astype(jnp.bfloat16)
        y = y + jnp.dot(hid, wout_ref[f0:f0 + fc, :], preferred_element_type=jnp.float32)
    o_ref[...] = _layer_norm(alpha * x + 0.5 * gate * y, g_ref[...], b_ref[...])


def _half_step_ffn(xt, mod, ln_g, ln_b, w_in, w_out, *, layer, which, mod_row, tm, alpha):
    nb, s, d = xt.shape
    f = w_out.shape[2]
    kern = functools.partial(_ffn_kernel, k0=0 if which == 0 else 6, alpha=alpha, f_chunk=F_CHUNK)
    ln_k = 0 if which == 0 else 2
    return pl.pallas_call(
        kern,
        grid=(nb, s // tm),
        in_specs=[_tok_spec(tm, d), _mod_spec(layer, mod_row, d),
                  _resident((None, None, d, 2 * f), lambda b, t: (layer, which, 0, 0)),
                  _resident((None, None, f, d), lambda b, t: (layer, which, 0, 0)),
                  _ln_spec(layer, ln_k, d), _ln_spec(layer, ln_k, d)],
        out_specs=_tok_spec(tm, d),
        out_shape=jax.ShapeDtypeStruct(xt.shape, jnp.float32),
        compiler_params=_cparams(2),
        name="half_step_ffn",
    )(xt, mod, w_in, w_out, ln_g, ln_b)


def _rope_slab(v, cos, sin, first_half):
    up = pltpu.roll(v, LANES - 16, 1)
    dn = pltpu.roll(v, 16, 1)
    return v * cos + jnp.where(first_half, up, dn) * sin


def _qkv_kernel(x_ref, mod_ref, w_ref, *rest, rope, q_scale):
    if rope:
        cos_ref, sin_ref, q_ref, k_ref, v_ref = rest
    else:
        q_ref, k_ref, v_ref = rest
    x = x_ref[...]
    d = x.shape[1]
    h = (x * (1 + mod_ref[4:5, :]) + mod_ref[3:4, :]).astype(jnp.bfloat16)
    v_ref[...] = jnp.dot(h, w_ref[:, 2 * d:], preferred_element_type=jnp.float32).astype(v_ref.dtype)
    for out_ref, c0, mult in ((q_ref, 0, q_scale), (k_ref, d, None)):
        y = jnp.dot(h, w_ref[:, c0:c0 + d], preferred_element_type=jnp.float32)
        if mult is not None:
            y = y * mult
        if rope:
            cos, sin = cos_ref[...], sin_ref[...]
            lane = lax.broadcasted_iota(jnp.int32, (1, LANES), 1)
            first_half = (lane % 32) < 16
            for s0 in range(0, d, LANES):
                out_ref[:, s0:s0 + LANES] = _rope_slab(y[:, s0:s0 + LANES], cos, sin,
                                                       first_half).astype(out_ref.dtype)
        else:
            out_ref[...] = y.astype(out_ref.dtype)


def _qkv_projection(xt, mod, w_qkv, rope_tabs, *, layer, j, mod_row, tm, q_scale):
    nb, s, d = xt.shape
    rope = rope_tabs is not None
    in_specs = [_tok_spec(tm, d), _mod_spec(layer, mod_row, d),
                _resident((None, d, 3 * d), lambda b, t: (j, 0, 0))]
    args = [xt, mod, w_qkv]
    if rope:
        in_specs += [pl.BlockSpec((tm, LANES), lambda b, t: (t, 0))] * 2
        args += list(rope_tabs)
    out = jax.ShapeDtypeStruct(xt.shape, jnp.bfloat16)
    return pl.pallas_call(
        functools.partial(_qkv_kernel, rope=rope, q_scale=q_scale),
        grid=(nb, s // tm),
        in_specs=in_specs,
        out_specs=[_tok_spec(tm, d)] * 3,
        out_shape=[out] * 3,
        compiler_params=_cparams(2),
        name="qkv_projection",
    )(*args)


def _stack_heads(q, head0):
    zero = jnp.zeros_like(q)
    return jnp.concatenate([jnp.where(head0, q, zero), jnp.where(head0, zero, q)], axis=0)


def _unstack_heads(o, head0):
    n = o.shape[0] // 2
    return jnp.where(head0, o[:n], o[n:])


def _dot_nt(a, b):
    return lax.dot_general(a, b, (((1,), (1,)), ((), ())), preferred_element_type=jnp.float32)


def _na_kernel(q_ref, k_ref, v_ref, kc_ref, vc_ref, bias_ref, *rest, rows, ctx_queries):
    if ctx_queries:
        qc_ref, o_ref, oc_ref = rest
    else:
        (o_ref,) = rest
    head0 = lax.broadcasted_iota(jnp.int32, (1, LANES), 1) < (LANES // HEADS_PER_SLAB)
    kc, vc = kc_ref[...], vc_ref[...]
    nk = NA_KH * GRID_W

    def row_body(r, carry):
        rs = jnp.clip(r - NA_KH // 2, 0, rows - NA_KH)
        q0 = pl.multiple_of(r * GRID_W, GRID_W)
        k0 = pl.multiple_of(rs * GRID_W, GRID_W)
        qs = _stack_heads(q_ref[pl.ds(q0, GRID_W), :], head0)
        dy0 = (NA_KH - 1) - (r - rs)
        bias = jnp.concatenate([bias_ref[dy0 + 2 * m] for m in range(NA_KH // 2)], axis=1)
        s_win = _dot_nt(qs, k_ref[pl.ds(k0, nk), :]) + bias
        s_ctx = _dot_nt(qs, kc)
        m = jnp.maximum(jnp.max(s_win, axis=-1, keepdims=True), jnp.max(s_ctx, axis=-1, keepdims=True))
        p_win, p_ctx = jnp.exp(s_win - m), jnp.exp(s_ctx - m)
        den = jnp.sum(p_win, axis=-1, keepdims=True) + jnp.sum(p_ctx, axis=-1, keepdims=True)
        o = (jnp.dot(p_win.astype(jnp.bfloat16), v_ref[pl.ds(k0, nk), :], preferred_element_type=jnp.float32)
             + jnp.dot(p_ctx.astype(jnp.bfloat16), vc, preferred_element_type=jnp.float32))
        o_ref[pl.ds(q0, GRID_W), :] = _unstack_heads(o / den, head0).astype(o_ref.dtype)
        return carry

    lax.fori_loop(0, rows, row_body, 0, unroll=NA_ROW_UNROLL)

    if ctx_queries:
        s = _dot_nt(_stack_heads(qc_ref[...], head0), kc)
        p = jnp.exp(s - jnp.max(s, axis=-1, keepdims=True))
        den = jnp.sum(p, axis=-1, keepdims=True)
        o = jnp.dot(p.astype(jnp.bfloat16), vc, preferred_element_type=jnp.float32)
        oc_ref[...] = _unstack_heads(o / den, head0).astype(oc_ref.dtype)


def _na_bias_table(rpb):
    h, n_dy, _ = rpb.shape
    col = np.arange(GRID_W)
    win_c0 = np.clip(col - NA_KW // 2, 0, GRID_W - NA_KW)
    allowed = (col[None, :] >= win_c0[:, None]) & (col[None, :] < win_c0[:, None] + NA_KW)
    period = 2 * GRID_W
    fill = jnp.full((h, n_dy, period - (2 * NA_KW - 1)), MASKED, rpb.dtype)
    seq = jnp.concatenate([rpb[..., NA_KW - 1:], fill, rpb[..., :NA_KW - 1]], axis=-1)
    flat = jnp.tile(seq, (1, 1, GRID_W))[..., :GRID_W * (period - 1)]
    toe = flat.reshape(h, n_dy, GRID_W, period - 1)[..., :GRID_W]
    toe = jnp.where(jnp.asarray(allowed), toe, MASKED)
    two = jnp.concatenate([toe[:, :-1], toe[:, 1:]], axis=-1)
    two = two.reshape(h // HEADS_PER_SLAB, HEADS_PER_SLAB, n_dy - 1, GRID_W, 2 * GRID_W)
    return jnp.moveaxis(two, 1, 2).reshape(h // HEADS_PER_SLAB, n_dy - 1, HEADS_PER_SLAB * GRID_W, 2 * GRID_W)


def _neighbourhood_attention(q, k, v, qc, kc, vc, bias, *, ctx_queries):
    nb, s, d = q.shape
    n_ctx = kc.shape[1]
    slabs = d // LANES
    rows = s // GRID_W
    lat = pl.BlockSpec((None, s, LANES), lambda p, b: (b, 0, p))
    ctx = pl.BlockSpec((None, n_ctx, LANES), lambda p, b: (b, 0, p))
    in_specs = [lat, lat, lat, ctx, ctx,
                pl.BlockSpec((None,) + bias.shape[1:], lambda p, b: (p, 0, 0, 0))]
    args = [q, k, v, kc, vc, bias]
    out_specs, out_shape = [lat], [jax.ShapeDtypeStruct(q.shape, jnp.bfloat16)]
    if ctx_queries:
        in_specs.append(ctx)
        args.append(qc)
        out_specs.append(ctx)
        out_shape.append(jax.ShapeDtypeStruct(kc.shape, jnp.bfloat16))
    outs = pl.pallas_call(
        functools.partial(_na_kernel, rows=rows, ctx_queries=ctx_queries),
        grid=(slabs, nb),
        in_specs=in_specs, out_specs=out_specs, out_shape=out_shape,
        compiler_params=_cparams(2),
        name="neighbourhood_attention",
    )(*args)
    return (outs[0], outs[1]) if ctx_queries else (outs[0], None)


def _mix_out_kernel(x_ref, o_ref, mod_ref, w_ref, g_ref, b_ref, out_ref, *, alpha):
    y = jnp.dot(o_ref[...], w_ref[...], preferred_element_type=jnp.float32)
    out_ref[...] = _layer_norm(alpha * x_ref[...] + 1.0 * mod_ref[5:6, :] * y, g_ref[...], b_ref[...])


def _mixer_out(xt, o, mod, ln_g, ln_b, w_out, *, layer, j, mod_row, tm, alpha):
    nb, s, d = xt.shape
    return pl.pallas_call(
        functools.partial(_mix_out_kernel, alpha=alpha),
        grid=(nb, s // tm),
        in_specs=[_tok_spec(tm, d), _tok_spec(tm, d), _mod_spec(layer, mod_row, d),
                  _resident((None, d, d), lambda b, t: (j, 0, 0)),
                  _ln_spec(layer, 1, d), _ln_spec(layer, 1, d)],
        out_specs=_tok_spec(tm, d),
        out_shape=jax.ShapeDtypeStruct(xt.shape, jnp.float32),
        compiler_params=_cparams(2),
        name="mixer_out",
    )(xt, o, mod, w_out, ln_g, ln_b)


def _conv_in_kernel(x_ref, mod_ref, w_ref, bg_ref, u_ref):
    x = x_ref[...]
    d = x.shape[1]
    h = (x * (1 + mod_ref[4:5, :]) + mod_ref[3:4, :]).astype(jnp.bfloat16)
    bg_ref[...] = jnp.dot(h, w_ref[:, :d], preferred_element_type=jnp.float32)
    cg = jnp.dot(h, w_ref[:, d:2 * d], preferred_element_type=jnp.float32)
    hv = jnp.dot(h, w_ref[:, 2 * d:], preferred_element_type=jnp.float32)
    u_ref[...] = cg * hv


def _conv_in(xt, mod, w_in, *, layer, j, mod_row, tm):
    nb, s, d = xt.shape
    out = jax.ShapeDtypeStruct(xt.shape, jnp.float32)
    return pl.pallas_call(
        _conv_in_kernel,
        grid=(nb, s // tm),
        in_specs=[_tok_spec(tm, d), _mod_spec(layer, mod_row, d),
                  _resident((None, d, 3 * d), lambda b, t: (j, 0, 0))],
        out_specs=[_tok_spec(tm, d)] * 2,
        out_shape=[out] * 2,
        compiler_params=_cparams(2),
        name="conv_in",
    )(xt, mod, w_in)


def _conv_out_kernel(x_ref, bg_ref, u_ref, up_ref, un_ref, cw_ref, mod_ref, w_ref, g_ref, b_ref, out_ref, *, alpha):
    t, nt = pl.program_id(1), pl.num_programs(1)
    u = u_ref[...]
    tm = u.shape[0]
    row = lax.broadcasted_iota(jnp.int32, (tm, 1), 0)
    before = jnp.where(t > 0, up_ref[SUBLANES - 1:SUBLANES, :], 0.0)
    after = jnp.where(t < nt - 1, un_ref[0:1, :], 0.0)
    u_prev = jnp.where(row == 0, before, pltpu.roll(u, 1, 0))
    u_next = jnp.where(row == tm - 1, after, pltpu.roll(u, tm - 1, 0))
    y = cw_ref[0:1, :] * u_prev + cw_ref[1:2, :] * u + cw_ref[2:3, :] * u_next
    z = jnp.dot((bg_ref[...] * y).astype(jnp.bfloat16), w_ref[...], preferred_element_type=jnp.float32)
    out_ref[...] = _layer_norm(alpha * x_ref[...] + 1.0 * mod_ref[5:6, :] * z, g_ref[...], b_ref[...])


def _conv_out(xt, bg, u, conv_w, mod, ln_g, ln_b, w_out, *, layer, j, mod_row, tm, alpha):
    nb, s, d = xt.shape
    per_tile = tm // SUBLANES
    last = s // SUBLANES - 1
    prev_spec = pl.BlockSpec((None, SUBLANES, d), lambda b, t: (b, jnp.maximum(t * per_tile - 1, 0), 0))
    next_spec = pl.BlockSpec((None, SUBLANES, d), lambda b, t: (b, jnp.minimum((t + 1) * per_tile, last), 0))
    return pl.pallas_call(
        functools.partial(_conv_out_kernel, alpha=alpha),
        grid=(nb, s // tm),
        in_specs=[_tok_spec(tm, d), _tok_spec(tm, d), _tok_spec(tm, d), prev_spec, next_spec,
                  pl.BlockSpec((None, CONV_W, d), lambda b, t: (j, 0, 0)),
                  _mod_spec(layer, mod_row, d),
                  _resident((None, d, d), lambda b, t: (j, 0, 0)),
                  _ln_spec(layer, 1, d), _ln_spec(layer, 1, d)],
        out_specs=_tok_spec(tm, d),
        out_shape=jax.ShapeDtypeStruct(xt.shape, jnp.float32),
        compiler_params=_cparams(2),
        name="conv_out",
    )(xt, bg, u, u, u, conv_w, mod, w_out, ln_g, ln_b)


def _rope_tables(n_tok, head_dim):
    t = np.arange(n_tok)
    n_freq = head_dim // 4
    inv_freq = ROPE_BASE ** (-np.arange(n_freq) / n_freq)
    ang_row = (t // GRID_W)[:, None] * inv_freq[None]
    ang_col = (t % GRID_W)[:, None] * inv_freq[None]
    cos = np.concatenate([np.cos(ang_row)] * 2 + [np.cos(ang_col)] * 2, axis=1)
    sin = np.concatenate([-np.sin(ang_row), np.sin(ang_row), -np.sin(ang_col), np.sin(ang_col)], axis=1)
    reps = LANES // head_dim
    return (jnp.asarray(np.tile(cos, (1, reps)), jnp.float32),
            jnp.asarray(np.tile(sin, (1, reps)), jnp.float32))


def kernel(x, c, ctx, c_ctx, ada_w, ada_b, ln_g, ln_b, ffn_w_in, ffn_w_out, na_w_qkv, na_w_out, na_rpb,
           sc_w_in, sc_conv, sc_w_out):
    nb, s, d = x.shape
    n_ctx = ctx.shape[1]
    depth = ada_w.shape[0]
    head_dim = d // N_HEADS
    alpha = (2.0 * depth) ** 0.25
    q_scale = head_dim ** -0.5
    last_na = max(range(0, depth, N_MIXERS))
    assert s % TM_LATENT == 0 and TM_LATENT % GRID_W == 0 and s // GRID_W >= NA_KH
    assert LANES // head_dim == HEADS_PER_SLAB

    ctx_row = nb
    n_rows = -(-(nb + 1) // SUBLANES) * SUBLANES
    cc = jnp.concatenate([c, c_ctx[None], jnp.zeros((n_rows - nb - 1, d), c.dtype)], axis=0)
    mod = _ada_modulation(cc, ada_w, ada_b)

    bf = jnp.bfloat16
    ffn_w_in, ffn_w_out = ffn_w_in.astype(bf), ffn_w_out.astype(bf)
    na_w_qkv, na_w_out = na_w_qkv.astype(bf), na_w_out.astype(bf)
    sc_w_in, sc_w_out = sc_w_in.astype(bf), sc_w_out.astype(bf)
    ln_g4, ln_b4 = ln_g.reshape(depth, 3, 1, d), ln_b.reshape(depth, 3, 1, d)
    rope_tabs = _rope_tables(s, head_dim)

    lat = dict(mod_row=None, tm=TM_LATENT)
    cx = dict(mod_row=ctx_row, tm=n_ctx)

    for i in range(depth):
        ctx_in = i <= last_na
        ctx_out = i < last_na
        j = i // N_MIXERS
        ffn = functools.partial(_half_step_ffn, mod=mod, ln_g=ln_g4, ln_b=ln_b4, w_in=ffn_w_in,
                                w_out=ffn_w_out, layer=i, alpha=alpha)

        x = ffn(x, which=0, **lat)
        if ctx_in:
            ctx = ffn(ctx, which=0, **cx)

        if i % N_MIXERS == 0:
            q, k, v = _qkv_projection(x, mod, na_w_qkv, rope_tabs, layer=i, j=j, q_scale=q_scale, **lat)
            qc, kc, vc = _qkv_projection(ctx, mod, na_w_qkv, None, layer=i, j=j, q_scale=q_scale, **cx)
            o, oc = _neighbourhood_attention(q, k, v, qc, kc, vc, _na_bias_table(na_rpb[j]),
                                             ctx_queries=ctx_out)
            mix = functools.partial(_mixer_out, mod=mod, ln_g=ln_g4, ln_b=ln_b4, w_out=na_w_out,
                                    layer=i, j=j, alpha=alpha)
            x = mix(x, o, **lat)
            if ctx_out:
                ctx = mix(ctx, oc, **cx)
        else:
            mix = functools.partial(_conv_out, conv_w=sc_conv, mod=mod, ln_g=ln_g4, ln_b=ln_b4,
                                    w_out=sc_w_out, layer=i, j=j, alpha=alpha)
            bg, u = _conv_in(x, mod, sc_w_in, layer=i, j=j, **lat)
            x = mix(x, bg, u, **lat)
            if ctx_out:
                bg, u = _conv_in(ctx, mod, sc_w_in, layer=i, j=j, **cx)
                ctx = mix(ctx, bg, u, **cx)

        x = ffn(x, which=1, **lat)
        if ctx_out:
            ctx = ffn(ctx, which=1, **cx)
    return x
```

```python
import functools

import numpy as np
import jax
import jax.numpy as jnp
from jax import lax
from jax.experimental import pallas as pl
from jax.experimental.pallas import tpu as pltpu

GRID_W = 64
N_HEADS = 16
N_MIXERS = 2
NA_KH = 8
NA_KW = 16
CONV_W = 3
ROPE_BASE = 10000.0
N_MOD = 9
LN_EPS = 1e-6
MASKED = -1e30

LANES = 128
SUBLANES = 8
HEADS_PER_SLAB = 2
VMEM_LIMIT = 56 * 1024 * 1024

TM_LATENT = 512
F_CHUNK = 256
NA_ROW_BLOCK = 2
NA_SOFTMAX_ROWS = 32


def _cparams(n_axes):
    return pltpu.CompilerParams(dimension_semantics=("arbitrary",) * n_axes,
                                vmem_limit_bytes=VMEM_LIMIT)


def _resident(block_shape, index_map):
    return pl.BlockSpec(block_shape, index_map, pipeline_mode=pl.Buffered(1))


def _layer_norm(z, g, b):
    mu = jnp.mean(z, axis=-1, keepdims=True)
    zc = z - mu
    var = jnp.mean(zc * zc, axis=-1, keepdims=True)
    return zc * lax.rsqrt(var + LN_EPS) * g + b


def _silu(a):
    return a * jax.nn.sigmoid(a)


def _ada_kernel(c_ref, w_ref, b_ref, o_ref):
    s = _silu(c_ref[...]).astype(jnp.bfloat16)
    o_ref[...] = jnp.dot(s, w_ref[...].astype(jnp.bfloat16),
                         preferred_element_type=jnp.float32) + b_ref[...]


def _ada_modulation(cc, ada_w, ada_b):
    depth, d, n = ada_w.shape
    r = cc.shape[0]
    tn = d
    out = pl.pallas_call(
        _ada_kernel,
        grid=(depth, n // tn),
        in_specs=[pl.BlockSpec((r, d), lambda i, j: (0, 0)),
                  pl.BlockSpec((None, d, tn), lambda i, j: (i, 0, j)),
                  pl.BlockSpec((None, 1, tn), lambda i, j: (i, 0, j))],
        out_specs=pl.BlockSpec((None, r, tn), lambda i, j: (i, 0, j)),
        out_shape=jax.ShapeDtypeStruct((depth, r, n), jnp.float32),
        compiler_params=_cparams(2),
        name="ada_modulation",
    )(cc, ada_w, ada_b.reshape(depth, 1, n))
    return out.reshape(depth, r, N_MOD, d)


def _tok_spec(tm, d):
    return pl.BlockSpec((None, tm, d), lambda b, t: (b, t, 0))


def _mod_spec(layer, mod_row, d):
    if mod_row is None:
        return pl.BlockSpec((None, None, N_MOD, d), lambda b, t: (layer, b, 0, 0))
    return pl.BlockSpec((None, None, N_MOD, d), lambda b, t: (layer, mod_row, 0, 0))


def _ln_spec(layer, k, d):
    return pl.BlockSpec((None, None, 1, d), lambda b, t: (layer, k, 0, 0))


def _ffn_kernel(x_ref, mod_ref, win_ref, wout_ref, g_ref, b_ref, o_ref, *, k0, alpha, f_chunk):
    x = x_ref[...]
    shift, scale, gate = mod_ref[k0:k0 + 1, :], mod_ref[k0 + 1:k0 + 2, :], mod_ref[k0 + 2:k0 + 3, :]
    h = (x * (1 + scale) + shift).astype(jnp.bfloat16)
    f = wout_ref.shape[0]
    y = jnp.zeros(x.shape, jnp.float32)
    for f0 in range(0, f, f_chunk):
        fc = min(f_chunk, f - f0)
        a = jnp.dot(h, win_ref[:, f0:f0 + fc], preferred_element_type=jnp.float32)
        u = jnp.dot(h, win_ref[:, f + f0:f + f0 + fc], preferred_element_type=jnp.float32)
        hid = (_silu(a) * u).astype(jnp.bfloat16)
        y = y + jnp.dot(hid, wout_ref[f0:f0 + fc, :], preferred_element_type=jnp.float32)
    o_ref[...] = _layer_norm(alpha * x + 0.5 * gate * y, g_ref[...], b_ref[...])


def _half_step_ffn(xt, mod, ln_g, ln_b, w_in, w_out, *, layer, which, mod_row, tm, alpha):
    nb, s, d = xt.shape
    f = w_out.shape[2]
    kern = functools.partial(_ffn_kernel, k0=0 if which == 0 else 6, alpha=alpha, f_chunk=F_CHUNK)
    ln_k = 0 if which == 0 else 2
    return pl.pallas_call(
        kern,
        grid=(nb, s // tm),
        in_specs=[_tok_spec(tm, d), _mod_spec(layer, mod_row, d),
                  _resident((None, None, d, 2 * f), lambda b, t: (layer, which, 0, 0)),
                  _resident((None, None, f, d), lambda b, t: (layer, which, 0, 0)),
                  _ln_spec(layer, ln_k, d), _ln_spec(layer, ln_k, d)],
        out_specs=_tok_spec(tm, d),
        out_shape=jax.ShapeDtypeStruct(xt.shape, jnp.float32),
        compiler_params=_cparams(2),
        name="half_step_ffn",
    )(xt, mod, w_in, w_out, ln_g, ln_b)


def _rope_slab(v, cos, sin, first_half):
    up = pltpu.roll(v, LANES - 16, 1)
    dn = pltpu.roll(v, 16, 1)
    return v * cos + jnp.where(first_half, up, dn) * sin


def _qkv_kernel(x_ref, mod_ref, w_ref, *rest, rope, q_scale):
    if rope:
        cos_ref, sin_ref, q_ref, k_ref, v_ref = rest
    else:
        q_ref, k_ref, v_ref = rest
    x = x_ref[...]
    d = x.shape[1]
    h = (x * (1 + mod_ref[4:5, :]) + mod_ref[3:4, :]).astype(jnp.bfloat16)
    v_ref[...] = jnp.dot(h, w_ref[:, 2 * d:], preferred_element_type=jnp.float32).astype(v_ref.dtype)
    for out_ref, c0, mult in ((q_ref, 0, q_scale), (k_ref, d, None)):
        y = jnp.dot(h, w_ref[:, c0:c0 + d], preferred_element_type=jnp.float32)
        if mult is not None:
            y = y * mult
        if rope:
            cos, sin = cos_ref[...], sin_ref[...]
            lane = lax.broadcasted_iota(jnp.int32, (1, LANES), 1)
            first_half = (lane % 32) < 16
            for s0 in range(0, d, LANES):
                out_ref[:, s0:s0 + LANES] = _rope_slab(y[:, s0:s0 + LANES], cos, sin,
                                                       first_half).astype(out_ref.dtype)
        else:
            out_ref[...] = y.astype(out_ref.dtype)


def _qkv_projection(xt, mod, w_qkv, rope_tabs, *, layer, j, mod_row, tm, q_scale):
    nb, s, d = xt.shape
    rope = rope_tabs is not None
    in_specs = [_tok_spec(tm, d), _mod_spec(layer, mod_row, d),
                _resident((None, d, 3 * d), lambda b, t: (j, 0, 0))]
    args = [xt, mod, w_qkv]
    if rope:
        in_specs += [pl.BlockSpec((tm, LANES), lambda b, t: (t, 0))] * 2
        args += list(rope_tabs)
    out = jax.ShapeDtypeStruct(xt.shape, jnp.bfloat16)
    return pl.pallas_call(
        functools.partial(_qkv_kernel, rope=rope, q_scale=q_scale),
        grid=(nb, s // tm),
        in_specs=in_specs,
        out_specs=[_tok_spec(tm, d)] * 3,
        out_shape=[out] * 3,
        compiler_params=_cparams(2),
        name="qkv_projection",
    )(*args)


def _stack_heads(q, head0):
    zero = jnp.zeros_like(q)
    return jnp.concatenate([jnp.where(head0, q, zero), jnp.where(head0, zero, q)], axis=0)


def _unstack_heads(o, head0):
    n = o.shape[0] // 2
    return jnp.where(head0, o[:n], o[n:])


def _dot_nt(a, b):
    return lax.dot_general(a, b, (((1,), (1,)), ((), ())), preferred_element_type=jnp.float32)


def _na_kernel(q_ref, k_ref, v_ref, kc_ref, vc_ref, bias_ref, *rest, rows, ctx_queries):
    if ctx_queries:
        qc_ref, o_ref, oc_ref, s_scr, p_scr, l_scr = rest
    else:
        o_ref, s_scr, p_scr, l_scr = rest
    head0 = lax.broadcasted_iota(jnp.int32, (1, LANES), 1) < (LANES // HEADS_PER_SLAB)
    nk = NA_KH * GRID_W
    n_stack = HEADS_PER_SLAB * GRID_W

    def window_start(r):
        rs = jnp.clip(r - NA_KH // 2, 0, rows - NA_KH)
        return rs, pl.multiple_of(rs * GRID_W, GRID_W)

    def scores(r):
        rs, k0 = window_start(r)
        qs = _stack_heads(q_ref[pl.ds(pl.multiple_of(r * GRID_W, GRID_W), GRID_W), :], head0)
        dy0 = (NA_KH - 1) - (r - rs)
        bias = jnp.concatenate([bias_ref[dy0 + 2 * m] for m in range(NA_KH // 2)], axis=1)
        s_scr[r, :, :nk] = _dot_nt(qs, k_ref[pl.ds(k0, nk), :]) + bias
        s_scr[r, :, nk:] = _dot_nt(qs, kc_ref[...])

    def softmax(r):
        for g0 in range(0, n_stack, NA_SOFTMAX_ROWS):
            s = s_scr[r, g0:g0 + NA_SOFTMAX_ROWS, :]
            p = jnp.exp(s - jnp.max(s, axis=-1, keepdims=True))
            l_scr[r, g0:g0 + NA_SOFTMAX_ROWS, :] = jnp.sum(p, axis=-1, keepdims=True)
            p_scr[r, g0:g0 + NA_SOFTMAX_ROWS, :] = p.astype(p_scr.dtype)

    def weighted_values(r):
        _, k0 = window_start(r)
        o = (jnp.dot(p_scr[r, :, :nk], v_ref[pl.ds(k0, nk), :], preferred_element_type=jnp.float32)
             + jnp.dot(p_scr[r, :, nk:], vc_ref[...], preferred_element_type=jnp.float32))
        o = _unstack_heads(o / l_scr[r], head0)
        o_ref[pl.ds(pl.multiple_of(r * GRID_W, GRID_W), GRID_W), :] = o.astype(o_ref.dtype)

    def on_block(stage, blk):
        for i in range(NA_ROW_BLOCK):
            stage(blk * NA_ROW_BLOCK + i)

    n_blk = rows // NA_ROW_BLOCK
    on_block(scores, 0)
    on_block(softmax, 0)
    on_block(scores, 1)

    def block_body(blk, carry):
        on_block(weighted_values, blk - 2)
        on_block(softmax, blk - 1)
        on_block(scores, blk)
        return carry

    lax.fori_loop(2, n_blk, block_body, 0)
    on_block(weighted_values, n_blk - 2)
    on_block(softmax, n_blk - 1)
    on_block(weighted_values, n_blk - 1)

    if ctx_queries:
        kc, vc = kc_ref[...], vc_ref[...]
        s = _dot_nt(_stack_heads(qc_ref[...], head0), kc)
        p = jnp.exp(s - jnp.max(s, axis=-1, keepdims=True))
        den = jnp.sum(p, axis=-1, keepdims=True)
        o = jnp.dot(p.astype(jnp.bfloat16), vc, preferred_element_type=jnp.float32)
        oc_ref[...] = _unstack_heads(o / den, head0).astype(oc_ref.dtype)


def _na_bias_table(rpb):
    h, n_dy, _ = rpb.shape
    col = np.arange(GRID_W)
    win_c0 = np.clip(col - NA_KW // 2, 0, GRID_W - NA_KW)
    allowed = (col[None, :] >= win_c0[:, None]) & (col[None, :] < win_c0[:, None] + NA_KW)
    period = 2 * GRID_W
    fill = jnp.full((h, n_dy, period - (2 * NA_KW - 1)), MASKED, rpb.dtype)
    seq = jnp.concatenate([rpb[..., NA_KW - 1:], fill, rpb[..., :NA_KW - 1]], axis=-1)
    flat = jnp.tile(seq, (1, 1, GRID_W))[..., :GRID_W * (period - 1)]
    toe = flat.reshape(h, n_dy, GRID_W, period - 1)[..., :GRID_W]
    toe = jnp.where(jnp.asarray(allowed), toe, MASKED)
    two = jnp.concatenate([toe[:, :-1], toe[:, 1:]], axis=-1)
    two = two.reshape(h // HEADS_PER_SLAB, HEADS_PER_SLAB, n_dy - 1, GRID_W, 2 * GRID_W)
    return jnp.moveaxis(two, 1, 2).reshape(h // HEADS_PER_SLAB, n_dy - 1, HEADS_PER_SLAB * GRID_W, 2 * GRID_W)


def _neighbourhood_attention(q, k, v, qc, kc, vc, bias, *, ctx_queries):
    nb, s, d = q.shape
    n_ctx = kc.shape[1]
    slabs = d // LANES
    rows = s // GRID_W
    n_stack = HEADS_PER_SLAB * GRID_W
    n_keys = NA_KH * GRID_W + n_ctx
    lat = pl.BlockSpec((None, s, LANES), lambda p, b: (b, 0, p))
    ctx = pl.BlockSpec((None, n_ctx, LANES), lambda p, b: (b, 0, p))
    in_specs = [lat, lat, lat, ctx, ctx,
                pl.BlockSpec((None,) + bias.shape[1:], lambda p, b: (p, 0, 0, 0))]
    args = [q, k, v, kc, vc, bias]
    out_specs, out_shape = [lat], [jax.ShapeDtypeStruct(q.shape, jnp.bfloat16)]
    if ctx_queries:
        in_specs.append(ctx)
        args.append(qc)
        out_specs.append(ctx)
        out_shape.append(jax.ShapeDtypeStruct(kc.shape, jnp.bfloat16))
    outs = pl.pallas_call(
        functools.partial(_na_kernel, rows=rows, ctx_queries=ctx_queries),
        grid=(slabs, nb),
        in_specs=in_specs, out_specs=out_specs, out_shape=out_shape,
        scratch_shapes=[pltpu.VMEM((rows, n_stack, n_keys), jnp.float32),
                        pltpu.VMEM((rows, n_stack, n_keys), jnp.bfloat16),
                        pltpu.VMEM((rows, n_stack, 1), jnp.float32)],
        compiler_params=_cparams(2),
        name="neighbourhood_attention",
    )(*args)
    return (outs[0], outs[1]) if ctx_queries else (outs[0], None)


def _mix_out_kernel(x_ref, o_ref, mod_ref, w_ref, g_ref, b_ref, out_ref, *, alpha):
    y = jnp.dot(o_ref[...], w_ref[...], preferred_element_type=jnp.float32)
    out_ref[...] = _layer_norm(alpha * x_ref[...] + 1.0 * mod_ref[5:6, :] * y, g_ref[...], b_ref[...])


def _mixer_out(xt, o, mod, ln_g, ln_b, w_out, *, layer, j, mod_row, tm, alpha):
    nb, s, d = xt.shape
    return pl.pallas_call(
        functools.partial(_mix_out_kernel, alpha=alpha),
        grid=(nb, s // tm),
        in_specs=[_tok_spec(tm, d), _tok_spec(tm, d), _mod_spec(layer, mod_row, d),
                  _resident((None, d, d), lambda b, t: (j, 0, 0)),
                  _ln_spec(layer, 1, d), _ln_spec(layer, 1, d)],
        out_specs=_tok_spec(tm, d),
        out_shape=jax.ShapeDtypeStruct(xt.shape, jnp.float32),
        compiler_params=_cparams(2),
        name="mixer_out",
    )(xt, o, mod, w_out, ln_g, ln_b)


def _conv_in_kernel(x_ref, mod_ref, w_ref, bg_ref, u_ref):
    x = x_ref[...]
    d = x.shape[1]
    h = (x * (1 + mod_ref[4:5, :]) + mod_ref[3:4, :]).astype(jnp.bfloat16)
    bg_ref[...] = jnp.dot(h, w_ref[:, :d], preferred_element_type=jnp.float32)
    cg = jnp.dot(h, w_ref[:, d:2 * d], preferred_element_type=jnp.float32)
    hv = jnp.dot(h, w_ref[:, 2 * d:], preferred_element_type=jnp.float32)
    u_ref[...] = cg * hv


def _conv_in(xt, mod, w_in, *, layer, j, mod_row, tm):
    nb, s, d = xt.shape
    out = jax.ShapeDtypeStruct(xt.shape, jnp.float32)
    return pl.pallas_call(
        _conv_in_kernel,
        grid=(nb, s // tm),
        in_specs=[_tok_spec(tm, d), _mod_spec(layer, mod_row, d),
                  _resident((None, d, 3 * d), lambda b, t: (j, 0, 0))],
        out_specs=[_tok_spec(tm, d)] * 2,
        out_shape=[out] * 2,
        compiler_params=_cparams(2),
        name="conv_in",
    )(xt, mod, w_in)


def _conv_out_kernel(x_ref, bg_ref, u_ref, up_ref, un_ref, cw_ref, mod_ref, w_ref, g_ref, b_ref, out_ref, *, alpha):
    t, nt = pl.program_id(1), pl.num_programs(1)
    u = u_ref[...]
    tm = u.shape[0]
    row = lax.broadcasted_iota(jnp.int32, (tm, 1), 0)
    before = jnp.where(t > 0, up_ref[SUBLANES - 1:SUBLANES, :], 0.0)
    after = jnp.where(t < nt - 1, un_ref[0:1, :], 0.0)
    u_prev = jnp.where(row == 0, before, pltpu.roll(u, 1, 0))
    u_next = jnp.where(row == tm - 1, after, pltpu.roll(u, tm - 1, 0))
    y = cw_ref[0:1, :] * u_prev + cw_ref[1:2, :] * u + cw_ref[2:3, :] * u_next
    z = jnp.dot((bg_ref[...] * y).astype(jnp.bfloat16), w_ref[...], preferred_element_type=jnp.float32)
    out_ref[...] = _layer_norm(alpha * x_ref[...] + 1.0 * mod_ref[5:6, :] * z, g_ref[...], b_ref[...])


def _conv_out(xt, bg, u, conv_w, mod, ln_g, ln_b, w_out, *, layer, j, mod_row, tm, alpha):
    nb, s, d = xt.shape
    per_tile = tm // SUBLANES
    last = s // SUBLANES - 1
    prev_spec = pl.BlockSpec((None, SUBLANES, d), lambda b, t: (b, jnp.maximum(t * per_tile - 1, 0), 0))
    next_spec = pl.BlockSpec((None, SUBLANES, d), lambda b, t: (b, jnp.minimum((t + 1) * per_tile, last), 0))
    return pl.pallas_call(
        functools.partial(_conv_out_kernel, alpha=alpha),
        grid=(nb, s // tm),
        in_specs=[_tok_spec(tm, d), _tok_spec(tm, d), _tok_spec(tm, d), prev_spec, next_spec,
                  pl.BlockSpec((None, CONV_W, d), lambda b, t: (j, 0, 0)),
                  _mod_spec(layer, mod_row, d),
                  _resident((None, d, d), lambda b, t: (j, 0, 0)),
                  _ln_spec(layer, 1, d), _ln_spec(layer, 1, d)],
        out_specs=_tok_spec(tm, d),
        out_shape=jax.ShapeDtypeStruct(xt.shape, jnp.float32),
        compiler_params=_cparams(2),
        name="conv_out",
    )(xt, bg, u, u, u, conv_w, mod, w_out, ln_g, ln_b)


def _rope_tables(n_tok, head_dim):
    t = np.arange(n_tok)
    n_freq = head_dim // 4
    inv_freq = ROPE_BASE ** (-np.arange(n_freq) / n_freq)
    ang_row = (t // GRID_W)[:, None] * inv_freq[None]
    ang_col = (t % GRID_W)[:, None] * inv_freq[None]
    cos = np.concatenate([np.cos(ang_row)] * 2 + [np.cos(ang_col)] * 2, axis=1)
    sin = np.concatenate([-np.sin(ang_row), np.sin(ang_row), -np.sin(ang_col), np.sin(ang_col)], axis=1)
    reps = LANES // head_dim
    return (jnp.asarray(np.tile(cos, (1, reps)), jnp.float32),
            jnp.asarray(np.tile(sin, (1, reps)), jnp.float32))


def kernel(x, c, ctx, c_ctx, ada_w, ada_b, ln_g, ln_b, ffn_w_in, ffn_w_out, na_w_qkv, na_w_out, na_rpb,
           sc_w_in, sc_conv, sc_w_out):
    nb, s, d = x.shape
    n_ctx = ctx.shape[1]
    depth = ada_w.shape[0]
    head_dim = d // N_HEADS
    alpha = (2.0 * depth) ** 0.25
    q_scale = head_dim ** -0.5
    last_na = max(range(0, depth, N_MIXERS))
    assert s % TM_LATENT == 0 and TM_LATENT % GRID_W == 0 and s // GRID_W >= NA_KH
    assert LANES // head_dim == HEADS_PER_SLAB

    ctx_row = nb
    n_rows = -(-(nb + 1) // SUBLANES) * SUBLANES
    cc = jnp.concatenate([c, c_ctx[None], jnp.zeros((n_rows - nb - 1, d), c.dtype)], axis=0)
    mod = _ada_modulation(cc, ada_w, ada_b)

    bf = jnp.bfloat16
    ffn_w_in, ffn_w_out = ffn_w_in.astype(bf), ffn_w_out.astype(bf)
    na_w_qkv, na_w_out = na_w_qkv.astype(bf), na_w_out.astype(bf)
    sc_w_in, sc_w_out = sc_w_in.astype(bf), sc_w_out.astype(bf)
    ln_g4, ln_b4 = ln_g.reshape(depth, 3, 1, d), ln_b.reshape(depth, 3, 1, d)
    rope_tabs = _rope_tables(s, head_dim)

    lat = dict(mod_row=None, tm=TM_LATENT)
    cx = dict(mod_row=ctx_row, tm=n_ctx)

    for i in range(depth):
        ctx_in = i <= last_na
        ctx_out = i < last_na
        j = i // N_MIXERS
        ffn = functools.partial(_half_step_ffn, mod=mod, ln_g=ln_g4, ln_b=ln_b4, w_in=ffn_w_in,
                                w_out=ffn_w_out, layer=i, alpha=alpha)

        x = ffn(x, which=0, **lat)
        if ctx_in:
            ctx = ffn(ctx, which=0, **cx)

        if i % N_MIXERS == 0:
            q, k, v = _qkv_projection(x, mod, na_w_qkv, rope_tabs, layer=i, j=j, q_scale=q_scale, **lat)
            qc, kc, vc = _qkv_projection(ctx, mod, na_w_qkv, None, layer=i, j=j, q_scale=q_scale, **cx)
            o, oc = _neighbourhood_attention(q, k, v, qc, kc, vc, _na_bias_table(na_rpb[j]),
                                             ctx_queries=ctx_out)
            mix = functools.partial(_mixer_out, mod=mod, ln_g=ln_g4, ln_b=ln_b4, w_out=na_w_out,
                                    layer=i, j=j, alpha=alpha)
            x = mix(x, o, **lat)
            if ctx_out:
                ctx = mix(ctx, oc, **cx)
        else:
            mix = functools.partial(_conv_out, conv_w=sc_conv, mod=mod, ln_g=ln_g4, ln_b=ln_b4,
                                    w_out=sc_w_out, layer=i, j=j, alpha=alpha)
            bg, u = _conv_in(x, mod, sc_w_in, layer=i, j=j, **lat)
            x = mix(x, bg, u, **lat)
            if ctx_out:
                bg, u = _conv_in(ctx, mod, sc_w_in, layer=i, j=j, **cx)
                ctx = mix(ctx, bg, u, **cx)

        x = ffn(x, which=1, **lat)
        if ctx_out:
            ctx = ffn(ctx, which=1, **cx)
    return x
```

```python
import functools

import numpy as np
import jax
import jax.numpy as jnp
from jax import lax
from jax.experimental import pallas as pl
from jax.experimental.pallas import tpu as pltpu

GRID_W = 64
N_HEADS = 16
N_MIXERS = 2
NA_KH = 8
NA_KW = 16
CONV_W = 3
ROPE_BASE = 10000.0
N_MOD = 9
LN_EPS = 1e-6
MASKED = -1e30

LANES = 128
SUBLANES = 8
HEADS_PER_SLAB = 2
VMEM_LIMIT = 56 * 1024 * 1024

TM_LATENT = 512
F_CHUNK = 256
NA_ROW_BLOCK = 2
NA_SOFTMAX_ROWS = 32


def _cparams(n_axes):
    return pltpu.CompilerParams(dimension_semantics=("arbitrary",) * n_axes,
                                vmem_limit_bytes=VMEM_LIMIT)


def _resident(block_shape, index_map):
    return pl.BlockSpec(block_shape, index_map, pipeline_mode=pl.Buffered(1))


def _layer_norm(z, g, b):
    mu = jnp.mean(z, axis=-1, keepdims=True)
    zc = z - mu
    var = jnp.mean(zc * zc, axis=-1, keepdims=True)
    return zc * lax.rsqrt(var + LN_EPS) * g + b


def _silu(a):
    return a * jax.nn.sigmoid(a)


def _modulated(x, mod_ref, k):
    return (x * (1 + mod_ref[k + 1:k + 2, :]) + mod_ref[k:k + 1, :]).astype(jnp.bfloat16)


def _swiglu_half_step(x, mod_ref, k0, win_ref, wout_ref, g, b, alpha):
    h = _modulated(x, mod_ref, k0)
    f = wout_ref.shape[0]
    y = jnp.zeros(x.shape, jnp.float32)
    for f0 in range(0, f, F_CHUNK):
        fc = min(F_CHUNK, f - f0)
        a = jnp.dot(h, win_ref[:, f0:f0 + fc], preferred_element_type=jnp.float32)
        u = jnp.dot(h, win_ref[:, f + f0:f + f0 + fc], preferred_element_type=jnp.float32)
        hid = (_silu(a) * u).astype(jnp.bfloat16)
        y = y + jnp.dot(hid, wout_ref[f0:f0 + fc, :], preferred_element_type=jnp.float32)
    return _layer_norm(alpha * x + 0.5 * mod_ref[k0 + 2:k0 + 3, :] * y, g, b)


def _ada_kernel(c_ref, w_ref, b_ref, o_ref):
    s = _silu(c_ref[...]).astype(jnp.bfloat16)
    o_ref[...] = jnp.dot(s, w_ref[...].astype(jnp.bfloat16),
                         preferred_element_type=jnp.float32) + b_ref[...]


def _ada_modulation(cc, ada_w, ada_b):
    depth, d, n = ada_w.shape
    r = cc.shape[0]
    tn = d
    out = pl.pallas_call(
        _ada_kernel,
        grid=(depth, n // tn),
        in_specs=[pl.BlockSpec((r, d), lambda i, j: (0, 0)),
                  pl.BlockSpec((None, d, tn), lambda i, j: (i, 0, j)),
                  pl.BlockSpec((None, 1, tn), lambda i, j: (i, 0, j))],
        out_specs=pl.BlockSpec((None, r, tn), lambda i, j: (i, 0, j)),
        out_shape=jax.ShapeDtypeStruct((depth, r, n), jnp.float32),
        compiler_params=_cparams(2),
        name="ada_modulation",
    )(cc, ada_w, ada_b.reshape(depth, 1, n))
    return out.reshape(depth, r, N_MOD, d)


def _tok_spec(tm, d):
    return pl.BlockSpec((None, tm, d), lambda b, t: (b, t, 0))


def _mod_spec(layer, mod_row, d):
    if mod_row is None:
        return pl.BlockSpec((None, None, N_MOD, d), lambda b, t: (layer, b, 0, 0))
    return pl.BlockSpec((None, None, N_MOD, d), lambda b, t: (layer, mod_row, 0, 0))


def _ln_spec(layer, k, d):
    return pl.BlockSpec((None, None, 1, d), lambda b, t: (layer, k, 0, 0))


def _ffn_specs(layer, which, d, f):
    return [_resident((None, None, d, 2 * f), lambda b, t: (layer, which, 0, 0)),
            _resident((None, None, f, d), lambda b, t: (layer, which, 0, 0))]


def _rope_slab(v, cos, sin, first_half):
    up = pltpu.roll(v, LANES - 16, 1)
    dn = pltpu.roll(v, 16, 1)
    return v * cos + jnp.where(first_half, up, dn) * sin


def _pre_mixer_kernel(x_ref, mod_ref, win_ref, wout_ref, g_ref, b_ref, wp_ref, *rest, mixer, rope, alpha, q_scale):
    if rope:
        cos_ref, sin_ref, x1_ref, *outs = rest
    else:
        x1_ref, *outs = rest
    x1 = _swiglu_half_step(x_ref[...], mod_ref, 0, win_ref, wout_ref, g_ref[...], b_ref[...], alpha)
    x1_ref[...] = x1
    d = x1.shape[1]
    h = _modulated(x1, mod_ref, 3)

    def proj(c0):
        return jnp.dot(h, wp_ref[:, c0:c0 + d], preferred_element_type=jnp.float32)

    if mixer == "conv":
        bg_ref, u_ref = outs
        bg_ref[...] = proj(0)
        u_ref[...] = proj(d) * proj(2 * d)
        return
    q_ref, k_ref, v_ref = outs
    v_ref[...] = proj(2 * d).astype(v_ref.dtype)
    for out_ref, c0, mult in ((q_ref, 0, q_scale), (k_ref, d, None)):
        y = proj(c0)
        if mult is not None:
            y = y * mult
        if rope:
            cos, sin = cos_ref[...], sin_ref[...]
            lane = lax.broadcasted_iota(jnp.int32, (1, LANES), 1)
            first_half = (lane % 32) < 16
            for s0 in range(0, d, LANES):
                out_ref[:, s0:s0 + LANES] = _rope_slab(y[:, s0:s0 + LANES], cos, sin,
                                                       first_half).astype(out_ref.dtype)
        else:
            out_ref[...] = y.astype(out_ref.dtype)


def _pre_mixer(xt, mod, ln_g, ln_b, ffn_w_in, ffn_w_out, w_proj, rope_tabs, *, mixer, layer, j, mod_row, tm,
               alpha, q_scale):
    nb, s, d = xt.shape
    f = ffn_w_out.shape[2]
    rope = rope_tabs is not None
    in_specs = ([_tok_spec(tm, d), _mod_spec(layer, mod_row, d)] + _ffn_specs(layer, 0, d, f)
                + [_ln_spec(layer, 0, d), _ln_spec(layer, 0, d),
                   _resident((None, d, 3 * d), lambda b, t: (j, 0, 0))])
    args = [xt, mod, ffn_w_in, ffn_w_out, ln_g, ln_b, w_proj]
    if rope:
        in_specs += [pl.BlockSpec((tm, LANES), lambda b, t: (t, 0))] * 2
        args += list(rope_tabs)
    f32 = jax.ShapeDtypeStruct(xt.shape, jnp.float32)
    bf16 = jax.ShapeDtypeStruct(xt.shape, jnp.bfloat16)
    out_shape = [f32, f32, f32] if mixer == "conv" else [f32, bf16, bf16, bf16]
    return pl.pallas_call(
        functools.partial(_pre_mixer_kernel, mixer=mixer, rope=rope, alpha=alpha, q_scale=q_scale),
        grid=(nb, s // tm),
        in_specs=in_specs,
        out_specs=[_tok_spec(tm, d)] * len(out_shape),
        out_shape=out_shape,
        compiler_params=_cparams(2),
        name="pre_mixer_" + mixer,
    )(*args)


def _stack_heads(q, head0):
    zero = jnp.zeros_like(q)
    return jnp.concatenate([jnp.where(head0, q, zero), jnp.where(head0, zero, q)], axis=0)


def _unstack_heads(o, head0):
    n = o.shape[0] // 2
    return jnp.where(head0, o[:n], o[n:])


def _dot_nt(a, b):
    return lax.dot_general(a, b, (((1,), (1,)), ((), ())), preferred_element_type=jnp.float32)


def _na_kernel(q_ref, k_ref, v_ref, kc_ref, vc_ref, bias_ref, *rest, rows, ctx_queries):
    if ctx_queries:
        qc_ref, o_ref, oc_ref, s_scr, p_scr, l_scr = rest
    else:
        o_ref, s_scr, p_scr, l_scr = rest
    head0 = lax.broadcasted_iota(jnp.int32, (1, LANES), 1) < (LANES // HEADS_PER_SLAB)
    nk = NA_KH * GRID_W
    n_stack = HEADS_PER_SLAB * GRID_W

    def window_start(r):
        rs = jnp.clip(r - NA_KH // 2, 0, rows - NA_KH)
        return rs, pl.multiple_of(rs * GRID_W, GRID_W)

    def scores(r):
        rs, k0 = window_start(r)
        qs = _stack_heads(q_ref[pl.ds(pl.multiple_of(r * GRID_W, GRID_W), GRID_W), :], head0)
        dy0 = (NA_KH - 1) - (r - rs)
        bias = jnp.concatenate([bias_ref[dy0 + 2 * m] for m in range(NA_KH // 2)], axis=1)
        s_scr[r, :, :nk] = _dot_nt(qs, k_ref[pl.ds(k0, nk), :]) + bias
        s_scr[r, :, nk:] = _dot_nt(qs, kc_ref[...])

    def softmax(r):
        for g0 in range(0, n_stack, NA_SOFTMAX_ROWS):
            s = s_scr[r, g0:g0 + NA_SOFTMAX_ROWS, :]
            p = jnp.exp(s - jnp.max(s, axis=-1, keepdims=True))
            l_scr[r, g0:g0 + NA_SOFTMAX_ROWS, :] = jnp.sum(p, axis=-1, keepdims=True)
            p_scr[r, g0:g0 + NA_SOFTMAX_ROWS, :] = p.astype(p_scr.dtype)

    def weighted_values(r):
        _, k0 = window_start(r)
        o = (jnp.dot(p_scr[r, :, :nk], v_ref[pl.ds(k0, nk), :], preferred_element_type=jnp.float32)
             + jnp.dot(p_scr[r, :, nk:], vc_ref[...], preferred_element_type=jnp.float32))
        o = _unstack_heads(o / l_scr[r], head0)
        o_ref[pl.ds(pl.multiple_of(r * GRID_W, GRID_W), GRID_W), :] = o.astype(o_ref.dtype)

    def on_block(stage, blk):
        for i in range(NA_ROW_BLOCK):
            stage(blk * NA_ROW_BLOCK + i)

    n_blk = rows // NA_ROW_BLOCK
    on_block(scores, 0)
    on_block(softmax, 0)
    on_block(scores, 1)

    def block_body(blk, carry):
        on_block(weighted_values, blk - 2)
        on_block(softmax, blk - 1)
        on_block(scores, blk)
        return carry

    lax.fori_loop(2, n_blk, block_body, 0)
    on_block(weighted_values, n_blk - 2)
    on_block(softmax, n_blk - 1)
    on_block(weighted_values, n_blk - 1)

    if ctx_queries:
        kc, vc = kc_ref[...], vc_ref[...]
        s = _dot_nt(_stack_heads(qc_ref[...], head0), kc)
        p = jnp.exp(s - jnp.max(s, axis=-1, keepdims=True))
        den = jnp.sum(p, axis=-1, keepdims=True)
        o = jnp.dot(p.astype(jnp.bfloat16), vc, preferred_element_type=jnp.float32)
        oc_ref[...] = _unstack_heads(o / den, head0).astype(oc_ref.dtype)


def _na_bias_table(rpb):
    h, n_dy, _ = rpb.shape
    col = np.arange(GRID_W)
    win_c0 = np.clip(col - NA_KW // 2, 0, GRID_W - NA_KW)
    allowed = (col[None, :] >= win_c0[:, None]) & (col[None, :] < win_c0[:, None] + NA_KW)
    period = 2 * GRID_W
    fill = jnp.full((h, n_dy, period - (2 * NA_KW - 1)), MASKED, rpb.dtype)
    seq = jnp.concatenate([rpb[..., NA_KW - 1:], fill, rpb[..., :NA_KW - 1]], axis=-1)
    flat = jnp.tile(seq, (1, 1, GRID_W))[..., :GRID_W * (period - 1)]
    toe = flat.reshape(h, n_dy, GRID_W, period - 1)[..., :GRID_W]
    toe = jnp.where(jnp.asarray(allowed), toe, MASKED)
    two = jnp.concatenate([toe[:, :-1], toe[:, 1:]], axis=-1)
    two = two.reshape(h // HEADS_PER_SLAB, HEADS_PER_SLAB, n_dy - 1, GRID_W, 2 * GRID_W)
    return jnp.moveaxis(two, 1, 2).reshape(h // HEADS_PER_SLAB, n_dy - 1, HEADS_PER_SLAB * GRID_W, 2 * GRID_W)


def _neighbourhood_attention(q, k, v, qc, kc, vc, bias, *, ctx_queries):
    nb, s, d = q.shape
    n_ctx = kc.shape[1]
    slabs = d // LANES
    rows = s // GRID_W
    n_stack = HEADS_PER_SLAB * GRID_W
    n_keys = NA_KH * GRID_W + n_ctx
    lat = pl.BlockSpec((None, s, LANES), lambda p, b: (b, 0, p))
    ctx = pl.BlockSpec((None, n_ctx, LANES), lambda p, b: (b, 0, p))
    in_specs = [lat, lat, lat, ctx, ctx,
                pl.BlockSpec((None,) + bias.shape[1:], lambda p, b: (p, 0, 0, 0))]
    args = [q, k, v, kc, vc, bias]
    out_specs, out_shape = [lat], [jax.ShapeDtypeStruct(q.shape, jnp.bfloat16)]
    if ctx_queries:
        in_specs.append(ctx)
        args.append(qc)
        out_specs.append(ctx)
        out_shape.append(jax.ShapeDtypeStruct(kc.shape, jnp.bfloat16))
    outs = pl.pallas_call(
        functools.partial(_na_kernel, rows=rows, ctx_queries=ctx_queries),
        grid=(slabs, nb),
        in_specs=in_specs, out_specs=out_specs, out_shape=out_shape,
        scratch_shapes=[pltpu.VMEM((rows, n_stack, n_keys), jnp.float32),
                        pltpu.VMEM((rows, n_stack, n_keys), jnp.bfloat16),
                        pltpu.VMEM((rows, n_stack, 1), jnp.float32)],
        compiler_params=_cparams(2),
        name="neighbourhood_attention",
    )(*args)
    return (outs[0], outs[1]) if ctx_queries else (outs[0], None)


def _post_mixer_kernel(x_ref, *rest, mixer, alpha):
    if mixer == "conv":
        bg_ref, u_ref, up_ref, un_ref, cw_ref, *rest = rest
    else:
        o_ref, *rest = rest
    mod_ref, wmix_ref, g1_ref, b1_ref, win_ref, wout_ref, g2_ref, b2_ref, out_ref = rest
    if mixer == "conv":
        t, nt = pl.program_id(1), pl.num_programs(1)
        u = u_ref[...]
        tm = u.shape[0]
        row = lax.broadcasted_iota(jnp.int32, (tm, 1), 0)
        before = jnp.where(t > 0, up_ref[SUBLANES - 1:SUBLANES, :], 0.0)
        after = jnp.where(t < nt - 1, un_ref[0:1, :], 0.0)
        u_prev = jnp.where(row == 0, before, pltpu.roll(u, 1, 0))
        u_next = jnp.where(row == tm - 1, after, pltpu.roll(u, tm - 1, 0))
        y = cw_ref[0:1, :] * u_prev + cw_ref[1:2, :] * u + cw_ref[2:3, :] * u_next
        mixed = (bg_ref[...] * y).astype(jnp.bfloat16)
    else:
        mixed = o_ref[...]
    z = jnp.dot(mixed, wmix_ref[...], preferred_element_type=jnp.float32)
    x2 = _layer_norm(alpha * x_ref[...] + 1.0 * mod_ref[5:6, :] * z, g1_ref[...], b1_ref[...])
    out_ref[...] = _swiglu_half_step(x2, mod_ref, 6, win_ref, wout_ref, g2_ref[...], b2_ref[...], alpha)


def _post_mixer(xt, mixer_in, mod, ln_g, ln_b, w_mix, ffn_w_in, ffn_w_out, conv_w, *, mixer, layer, j, mod_row, tm,
                alpha):
    nb, s, d = xt.shape
    f = ffn_w_out.shape[2]
    tok = _tok_spec(tm, d)
    if mixer == "conv":
        bg, u = mixer_in
        per_tile = tm // SUBLANES
        last = s // SUBLANES - 1
        prev_spec = pl.BlockSpec((None, SUBLANES, d), lambda b, t: (b, jnp.maximum(t * per_tile - 1, 0), 0))
        next_spec = pl.BlockSpec((None, SUBLANES, d), lambda b, t: (b, jnp.minimum((t + 1) * per_tile, last), 0))
        mix_specs = [tok, tok, prev_spec, next_spec, pl.BlockSpec((None, CONV_W, d), lambda b, t: (j, 0, 0))]
        mix_args = [bg, u, u, u, conv_w]
    else:
        mix_specs, mix_args = [tok], [mixer_in]
    return pl.pallas_call(
        functools.partial(_post_mixer_kernel, mixer=mixer, alpha=alpha),
        grid=(nb, s // tm),
        in_specs=([tok] + mix_specs
                  + [_mod_spec(layer, mod_row, d), _resident((None, d, d), lambda b, t: (j, 0, 0)),
                     _ln_spec(layer, 1, d), _ln_spec(layer, 1, d)]
                  + _ffn_specs(layer, 1, d, f) + [_ln_spec(layer, 2, d), _ln_spec(layer, 2, d)]),
        out_specs=tok,
        out_shape=jax.ShapeDtypeStruct(xt.shape, jnp.float32),
        compiler_params=_cparams(2),
        name="post_mixer_" + mixer,
    )(xt, *mix_args, mod, w_mix, ln_g, ln_b, ffn_w_in, ffn_w_out, ln_g, ln_b)


def _rope_tables(n_tok, head_dim):
    t = np.arange(n_tok)
    n_freq = head_dim // 4
    inv_freq = ROPE_BASE ** (-np.arange(n_freq) / n_freq)
    ang_row = (t // GRID_W)[:, None] * inv_freq[None]
    ang_col = (t % GRID_W)[:, None] * inv_freq[None]
    cos = np.concatenate([np.cos(ang_row)] * 2 + [np.cos(ang_col)] * 2, axis=1)
    sin = np.concatenate([-np.sin(ang_row), np.sin(ang_row), -np.sin(ang_col), np.sin(ang_col)], axis=1)
    reps = LANES // head_dim
    return (jnp.asarray(np.tile(cos, (1, reps)), jnp.float32),
            jnp.asarray(np.tile(sin, (1, reps)), jnp.float32))


def kernel(x, c, ctx, c_ctx, ada_w, ada_b, ln_g, ln_b, ffn_w_in, ffn_w_out, na_w_qkv, na_w_out, na_rpb,
           sc_w_in, sc_conv, sc_w_out):
    nb, s, d = x.shape
    n_ctx = ctx.shape[1]
    depth = ada_w.shape[0]
    head_dim = d // N_HEADS
    alpha = (2.0 * depth) ** 0.25
    q_scale = head_dim ** -0.5
    last_na = max(range(0, depth, N_MIXERS))
    rows = s // GRID_W
    assert s % TM_LATENT == 0 and TM_LATENT % GRID_W == 0
    assert rows >= NA_KH and rows % NA_ROW_BLOCK == 0 and rows // NA_ROW_BLOCK >= 3
    assert LANES // head_dim == HEADS_PER_SLAB

    ctx_row = nb
    n_rows = -(-(nb + 1) // SUBLANES) * SUBLANES
    cc = jnp.concatenate([c, c_ctx[None], jnp.zeros((n_rows - nb - 1, d), c.dtype)], axis=0)
    mod = _ada_modulation(cc, ada_w, ada_b)

    bf = jnp.bfloat16
    ffn_w_in, ffn_w_out = ffn_w_in.astype(bf), ffn_w_out.astype(bf)
    na_w_qkv, na_w_out = na_w_qkv.astype(bf), na_w_out.astype(bf)
    sc_w_in, sc_w_out = sc_w_in.astype(bf), sc_w_out.astype(bf)
    ln_g4, ln_b4 = ln_g.reshape(depth, 3, 1, d), ln_b.reshape(depth, 3, 1, d)
    rope_tabs = _rope_tables(s, head_dim)

    lat = dict(mod_row=None, tm=TM_LATENT)
    cx = dict(mod_row=ctx_row, tm=n_ctx)

    for i in range(depth):
        ctx_in = i <= last_na
        ctx_out = i < last_na
        j = i // N_MIXERS
        mixer = "na" if i % N_MIXERS == 0 else "conv"
        w_proj, w_mix = (na_w_qkv, na_w_out) if mixer == "na" else (sc_w_in, sc_w_out)
        pre = functools.partial(_pre_mixer, mod=mod, ln_g=ln_g4, ln_b=ln_b4, ffn_w_in=ffn_w_in,
                                ffn_w_out=ffn_w_out, w_proj=w_proj, mixer=mixer, layer=i, j=j, alpha=alpha,
                                q_scale=q_scale)
        post = functools.partial(_post_mixer, mod=mod, ln_g=ln_g4, ln_b=ln_b4, w_mix=w_mix, ffn_w_in=ffn_w_in,
                                 ffn_w_out=ffn_w_out, conv_w=sc_conv, mixer=mixer, layer=i, j=j, alpha=alpha)

        x, *x_mix = pre(x, rope_tabs=rope_tabs if mixer == "na" else None, **lat)
        if ctx_in:
            ctx, *c_mix = pre(ctx, rope_tabs=None, **cx)

        if mixer == "na":
            (q, k, v), (qc, kc, vc) = x_mix, c_mix
            x_mix, c_mix = _neighbourhood_attention(q, k, v, qc, kc, vc, _na_bias_table(na_rpb[j]),
                                                    ctx_queries=ctx_out)

        x = post(x, x_mix, **lat)
        if ctx_out:
            ctx = post(ctx, c_mix, **cx)
    return x
```

```python
import functools

import numpy as np
import jax
import jax.numpy as jnp
from jax import lax
from jax.experimental import pallas as pl
from jax.experimental.pallas import tpu as pltpu

GRID_W = 64
N_HEADS = 16
N_MIXERS = 2
NA_KH = 8
NA_KW = 16
CONV_W = 3
ROPE_BASE = 10000.0
N_MOD = 9
LN_EPS = 1e-6
MASKED = -1e30

LANES = 128
SUBLANES = 8
HEADS_PER_SLAB = 2
VMEM_LIMIT = 56 * 1024 * 1024

TM_LATENT = 512
F_CHUNK = 256
NA_ROW_BLOCK = 4
NA_SOFTMAX_ROWS = 32


def _cparams(n_axes):
    return pltpu.CompilerParams(dimension_semantics=("arbitrary",) * n_axes,
                                vmem_limit_bytes=VMEM_LIMIT)


def _resident(block_shape, index_map):
    return pl.BlockSpec(block_shape, index_map, pipeline_mode=pl.Buffered(1))


def _layer_norm(z, g, b):
    mu = jnp.mean(z, axis=-1, keepdims=True)
    zc = z - mu
    var = jnp.mean(zc * zc, axis=-1, keepdims=True)
    return zc * lax.rsqrt(var + LN_EPS) * g + b


def _silu(a):
    return a * jax.nn.sigmoid(a)


def _modulated(x, mod_ref, k):
    return (x * (1 + mod_ref[k + 1:k + 2, :]) + mod_ref[k:k + 1, :]).astype(jnp.bfloat16)


def _swiglu_half_step(x, mod_ref, k0, win_ref, wout_ref, g, b, alpha):
    h = _modulated(x, mod_ref, k0)
    f = wout_ref.shape[0]
    y = jnp.zeros(x.shape, jnp.float32)
    for f0 in range(0, f, F_CHUNK):
        fc = min(F_CHUNK, f - f0)
        a = jnp.dot(h, win_ref[:, f0:f0 + fc], preferred_element_type=jnp.float32)
        u = jnp.dot(h, win_ref[:, f + f0:f + f0 + fc], preferred_element_type=jnp.float32)
        hid = (_silu(a) * u).astype(jnp.bfloat16)
        y = y + jnp.dot(hid, wout_ref[f0:f0 + fc, :], preferred_element_type=jnp.float32)
    return _layer_norm(alpha * x + 0.5 * mod_ref[k0 + 2:k0 + 3, :] * y, g, b)


def _ada_kernel(c_ref, w_ref, b_ref, o_ref):
    s = _silu(c_ref[...]).astype(jnp.bfloat16)
    o_ref[...] = jnp.dot(s, w_ref[...].astype(jnp.bfloat16),
                         preferred_element_type=jnp.float32) + b_ref[...]


def _ada_modulation(cc, ada_w, ada_b):
    depth, d, n = ada_w.shape
    r = cc.shape[0]
    tn = d
    out = pl.pallas_call(
        _ada_kernel,
        grid=(depth, n // tn),
        in_specs=[pl.BlockSpec((r, d), lambda i, j: (0, 0)),
                  pl.BlockSpec((None, d, tn), lambda i, j: (i, 0, j)),
                  pl.BlockSpec((None, 1, tn), lambda i, j: (i, 0, j))],
        out_specs=pl.BlockSpec((None, r, tn), lambda i, j: (i, 0, j)),
        out_shape=jax.ShapeDtypeStruct((depth, r, n), jnp.float32),
        compiler_params=_cparams(2),
        name="ada_modulation",
    )(cc, ada_w, ada_b.reshape(depth, 1, n))
    return out.reshape(depth, r, N_MOD, d)


def _tok_spec(tm, d):
    return pl.BlockSpec((None, tm, d), lambda b, t: (b, t, 0))


def _mod_spec(layer, mod_row, d):
    if mod_row is None:
        return pl.BlockSpec((None, None, N_MOD, d), lambda b, t: (layer, b, 0, 0))
    return pl.BlockSpec((None, None, N_MOD, d), lambda b, t: (layer, mod_row, 0, 0))


def _ln_spec(layer, k, d):
    return pl.BlockSpec((None, None, 1, d), lambda b, t: (layer, k, 0, 0))


def _ffn_specs(layer, which, d, f):
    return [_resident((None, None, d, 2 * f), lambda b, t: (layer, which, 0, 0)),
            _resident((None, None, f, d), lambda b, t: (layer, which, 0, 0))]


def _rope_slab(v, cos, sin, first_half):
    up = pltpu.roll(v, LANES - 16, 1)
    dn = pltpu.roll(v, 16, 1)
    return v * cos + jnp.where(first_half, up, dn) * sin


def _pre_mixer_kernel(x_ref, mod_ref, win_ref, wout_ref, g_ref, b_ref, wp_ref, *rest, mixer, rope, alpha, q_scale):
    if rope:
        cos_ref, sin_ref, x1_ref, *outs = rest
    else:
        x1_ref, *outs = rest
    x1 = _swiglu_half_step(x_ref[...], mod_ref, 0, win_ref, wout_ref, g_ref[...], b_ref[...], alpha)
    x1_ref[...] = x1
    d = x1.shape[1]
    h = _modulated(x1, mod_ref, 3)

    def proj(c0):
        return jnp.dot(h, wp_ref[:, c0:c0 + d], preferred_element_type=jnp.float32)

    if mixer == "conv":
        bg_ref, u_ref = outs
        bg_ref[...] = proj(0)
        u_ref[...] = proj(d) * proj(2 * d)
        return
    q_ref, k_ref, v_ref = outs
    vv = proj(2 * d).astype(v_ref.dtype)
    ones = jnp.ones((vv.shape[0], LANES), v_ref.dtype)
    for p in range(d // LANES):
        v_ref[:, 2 * p * LANES:(2 * p + 1) * LANES] = vv[:, p * LANES:(p + 1) * LANES]
        v_ref[:, (2 * p + 1) * LANES:(2 * p + 2) * LANES] = ones
    for out_ref, c0, mult in ((q_ref, 0, q_scale), (k_ref, d, None)):
        y = proj(c0)
        if mult is not None:
            y = y * mult
        if rope:
            cos, sin = cos_ref[...], sin_ref[...]
            lane = lax.broadcasted_iota(jnp.int32, (1, LANES), 1)
            first_half = (lane % 32) < 16
            for s0 in range(0, d, LANES):
                out_ref[:, s0:s0 + LANES] = _rope_slab(y[:, s0:s0 + LANES], cos, sin,
                                                       first_half).astype(out_ref.dtype)
        else:
            out_ref[...] = y.astype(out_ref.dtype)


def _pre_mixer(xt, mod, ln_g, ln_b, ffn_w_in, ffn_w_out, w_proj, rope_tabs, *, mixer, layer, j, mod_row, tm,
               alpha, q_scale):
    nb, s, d = xt.shape
    f = ffn_w_out.shape[2]
    rope = rope_tabs is not None
    in_specs = ([_tok_spec(tm, d), _mod_spec(layer, mod_row, d)] + _ffn_specs(layer, 0, d, f)
                + [_ln_spec(layer, 0, d), _ln_spec(layer, 0, d),
                   _resident((None, d, 3 * d), lambda b, t: (j, 0, 0))])
    args = [xt, mod, ffn_w_in, ffn_w_out, ln_g, ln_b, w_proj]
    if rope:
        in_specs += [pl.BlockSpec((tm, LANES), lambda b, t: (t, 0))] * 2
        args += list(rope_tabs)
    f32 = jax.ShapeDtypeStruct(xt.shape, jnp.float32)
    bf16 = jax.ShapeDtypeStruct(xt.shape, jnp.bfloat16)
    v_ext = jax.ShapeDtypeStruct((nb, s, 2 * d), jnp.bfloat16)
    out_shape = [f32, f32, f32] if mixer == "conv" else [f32, bf16, bf16, v_ext]
    return pl.pallas_call(
        functools.partial(_pre_mixer_kernel, mixer=mixer, rope=rope, alpha=alpha, q_scale=q_scale),
        grid=(nb, s // tm),
        in_specs=in_specs,
        out_specs=[_tok_spec(tm, o.shape[2]) for o in out_shape],
        out_shape=out_shape,
        compiler_params=_cparams(2),
        name="pre_mixer_" + mixer,
    )(*args)


def _stack_heads(q, head0):
    zero = jnp.zeros_like(q)
    return jnp.concatenate([jnp.where(head0, q, zero), jnp.where(head0, zero, q)], axis=0)


def _unstack_heads(o, head0):
    n = o.shape[0] // 2
    return jnp.where(head0, o[:n], o[n:])


def _dot_nt(a, b):
    return lax.dot_general(a, b, (((1,), (1,)), ((), ())), preferred_element_type=jnp.float32)


def _na_kernel(q_ref, k_ref, v_ref, kc_ref, vc_ref, bias_ref, *rest, rows, ctx_queries):
    if ctx_queries:
        qc_ref, o_ref, oc_ref, s_scr, p_scr = rest
    else:
        o_ref, s_scr, p_scr = rest
    head0 = lax.broadcasted_iota(jnp.int32, (1, LANES), 1) < (LANES // HEADS_PER_SLAB)
    nk = NA_KH * GRID_W
    n_stack = HEADS_PER_SLAB * GRID_W
    n_ctx = kc_ref.shape[0]

    def window_start(r):
        rs = min(max(r - NA_KH // 2, 0), rows - NA_KH)
        return rs, rs * GRID_W

    def scores(blk):
        r0 = blk * NA_ROW_BLOCK
        q_blk = q_ref[r0 * GRID_W:(r0 + NA_ROW_BLOCK) * GRID_W, :]
        zero = jnp.zeros_like(q_blk)
        q_heads = (jnp.where(head0, q_blk, zero), jnp.where(head0, zero, q_blk))
        for hh, q_head in enumerate(q_heads):
            s_ctx = _dot_nt(q_head, kc_ref[...]).reshape(NA_ROW_BLOCK, GRID_W, n_ctx)
            s_scr[r0:r0 + NA_ROW_BLOCK, hh * GRID_W:(hh + 1) * GRID_W, nk:] = s_ctx
        for i in range(NA_ROW_BLOCK):
            r = r0 + i
            rs, k0 = window_start(r)
            qs = jnp.concatenate([q_head[i * GRID_W:(i + 1) * GRID_W] for q_head in q_heads], axis=0)
            dy0 = (NA_KH - 1) - (r - rs)
            bias = jnp.concatenate([bias_ref[dy0 + 2 * m] for m in range(NA_KH // 2)], axis=1)
            s_scr[r, :, :nk] = _dot_nt(qs, k_ref[k0:k0 + nk, :]) + bias

    def softmax(blk):
        for r in range(blk * NA_ROW_BLOCK, (blk + 1) * NA_ROW_BLOCK):
            for g0 in range(0, n_stack, NA_SOFTMAX_ROWS):
                s = s_scr[r, g0:g0 + NA_SOFTMAX_ROWS, :]
                z = (s - jnp.max(s, axis=-1, keepdims=True)).astype(p_scr.dtype)
                p_scr[r, g0:g0 + NA_SOFTMAX_ROWS, :] = jnp.exp(z)

    def weighted_values(blk):
        r0 = blk * NA_ROW_BLOCK
        p_ctx = p_scr[r0:r0 + NA_ROW_BLOCK, :, nk:].reshape(NA_ROW_BLOCK * n_stack, n_ctx)
        o_ctx = jnp.dot(p_ctx, vc_ref[...], preferred_element_type=jnp.float32)
        for i in range(NA_ROW_BLOCK):
            r = r0 + i
            _, k0 = window_start(r)
            o = (jnp.dot(p_scr[r, :, :nk], v_ref[k0:k0 + nk, :], preferred_element_type=jnp.float32)
                 + o_ctx[i * n_stack:(i + 1) * n_stack])
            o = _unstack_heads(o[:, :LANES] / o[:, LANES:], head0)
            o_ref[r * GRID_W:(r + 1) * GRID_W, :] = o.astype(o_ref.dtype)

    n_blk = rows // NA_ROW_BLOCK
    for step in range(n_blk + 2):
        if step >= 2:
            weighted_values(step - 2)
        if 1 <= step <= n_blk:
            softmax(step - 1)
        if step < n_blk:
            scores(step)

    if ctx_queries:
        s = _dot_nt(_stack_heads(qc_ref[...], head0), kc_ref[...])
        p = jnp.exp((s - jnp.max(s, axis=-1, keepdims=True)).astype(jnp.bfloat16))
        o = jnp.dot(p, vc_ref[...], preferred_element_type=jnp.float32)
        oc_ref[...] = _unstack_heads(o[:, :LANES] / o[:, LANES:], head0).astype(oc_ref.dtype)


def _na_bias_table(rpb):
    h, n_dy, _ = rpb.shape
    col = np.arange(GRID_W)
    win_c0 = np.clip(col - NA_KW // 2, 0, GRID_W - NA_KW)
    allowed = (col[None, :] >= win_c0[:, None]) & (col[None, :] < win_c0[:, None] + NA_KW)
    period = 2 * GRID_W
    fill = jnp.full((h, n_dy, period - (2 * NA_KW - 1)), MASKED, rpb.dtype)
    seq = jnp.concatenate([rpb[..., NA_KW - 1:], fill, rpb[..., :NA_KW - 1]], axis=-1)
    flat = jnp.tile(seq, (1, 1, GRID_W))[..., :GRID_W * (period - 1)]
    toe = flat.reshape(h, n_dy, GRID_W, period - 1)[..., :GRID_W]
    toe = jnp.where(jnp.asarray(allowed), toe, MASKED)
    two = jnp.concatenate([toe[:, :-1], toe[:, 1:]], axis=-1)
    two = two.reshape(h // HEADS_PER_SLAB, HEADS_PER_SLAB, n_dy - 1, GRID_W, 2 * GRID_W)
    return jnp.moveaxis(two, 1, 2).reshape(h // HEADS_PER_SLAB, n_dy - 1, HEADS_PER_SLAB * GRID_W, 2 * GRID_W)


def _neighbourhood_attention(q, k, v, qc, kc, vc, bias, *, ctx_queries):
    nb, s, d = q.shape
    n_ctx = kc.shape[1]
    slabs = d // LANES
    rows = s // GRID_W
    n_stack = HEADS_PER_SLAB * GRID_W
    n_keys = NA_KH * GRID_W + n_ctx
    lat = pl.BlockSpec((None, s, LANES), lambda p, b: (b, 0, p))
    ctx = pl.BlockSpec((None, n_ctx, LANES), lambda p, b: (b, 0, p))
    lat_v = pl.BlockSpec((None, s, 2 * LANES), lambda p, b: (b, 0, p))
    ctx_v = pl.BlockSpec((None, n_ctx, 2 * LANES), lambda p, b: (b, 0, p))
    in_specs = [lat, lat, lat_v, ctx, ctx_v,
                pl.BlockSpec((None,) + bias.shape[1:], lambda p, b: (p, 0, 0, 0))]
    args = [q, k, v, kc, vc, bias]
    out_specs, out_shape = [lat], [jax.ShapeDtypeStruct(q.shape, jnp.bfloat16)]
    if ctx_queries:
        in_specs.append(ctx)
        args.append(qc)
        out_specs.append(ctx)
        out_shape.append(jax.ShapeDtypeStruct(qc.shape, jnp.bfloat16))
    outs = pl.pallas_call(
        functools.partial(_na_kernel, rows=rows, ctx_queries=ctx_queries),
        grid=(slabs, nb),
        in_specs=in_specs, out_specs=out_specs, out_shape=out_shape,
        scratch_shapes=[pltpu.VMEM((rows, n_stack, n_keys), jnp.float32),
                        pltpu.VMEM((rows, n_stack, n_keys), jnp.bfloat16)],
        compiler_params=_cparams(2),
        name="neighbourhood_attention",
    )(*args)
    return (outs[0], outs[1]) if ctx_queries else (outs[0], None)


def _post_mixer_kernel(x_ref, *rest, mixer, alpha):
    if mixer == "conv":
        bg_ref, u_ref, up_ref, un_ref, cw_ref, *rest = rest
    else:
        o_ref, *rest = rest
    mod_ref, wmix_ref, g1_ref, b1_ref, win_ref, wout_ref, g2_ref, b2_ref, out_ref = rest
    if mixer == "conv":
        t, nt = pl.program_id(1), pl.num_programs(1)
        u = u_ref[...]
        tm = u.shape[0]
        row = lax.broadcasted_iota(jnp.int32, (tm, 1), 0)
        before = jnp.where(t > 0, up_ref[SUBLANES - 1:SUBLANES, :], 0.0)
        after = jnp.where(t < nt - 1, un_ref[0:1, :], 0.0)
        u_prev = jnp.where(row == 0, before, pltpu.roll(u, 1, 0))
        u_next = jnp.where(row == tm - 1, after, pltpu.roll(u, tm - 1, 0))
        y = cw_ref[0:1, :] * u_prev + cw_ref[1:2, :] * u + cw_ref[2:3, :] * u_next
        mixed = (bg_ref[...] * y).astype(jnp.bfloat16)
    else:
        mixed = o_ref[...]
    z = jnp.dot(mixed, wmix_ref[...], preferred_element_type=jnp.float32)
    x2 = _layer_norm(alpha * x_ref[...] + 1.0 * mod_ref[5:6, :] * z, g1_ref[...], b1_ref[...])
    out_ref[...] = _swiglu_half_step(x2, mod_ref, 6, win_ref, wout_ref, g2_ref[...], b2_ref[...], alpha)


def _post_mixer(xt, mixer_in, mod, ln_g, ln_b, w_mix, ffn_w_in, ffn_w_out, conv_w, *, mixer, layer, j, mod_row, tm,
                alpha):
    nb, s, d = xt.shape
    f = ffn_w_out.shape[2]
    tok = _tok_spec(tm, d)
    if mixer == "conv":
        bg, u = mixer_in
        per_tile = tm // SUBLANES
        last = s // SUBLANES - 1
        prev_spec = pl.BlockSpec((None, SUBLANES, d), lambda b, t: (b, jnp.maximum(t * per_tile - 1, 0), 0))
        next_spec = pl.BlockSpec((None, SUBLANES, d), lambda b, t: (b, jnp.minimum((t + 1) * per_tile, last), 0))
        mix_specs = [tok, tok, prev_spec, next_spec, pl.BlockSpec((None, CONV_W, d), lambda b, t: (j, 0, 0))]
        mix_args = [bg, u, u, u, conv_w]
    else:
        mix_specs, mix_args = [tok], [mixer_in]
    return pl.pallas_call(
        functools.partial(_post_mixer_kernel, mixer=mixer, alpha=alpha),
        grid=(nb, s // tm),
        in_specs=([tok] + mix_specs
                  + [_mod_spec(layer, mod_row, d), _resident((None, d, d), lambda b, t: (j, 0, 0)),
                     _ln_spec(layer, 1, d), _ln_spec(layer, 1, d)]
                  + _ffn_specs(layer, 1, d, f) + [_ln_spec(layer, 2, d), _ln_spec(layer, 2, d)]),
        out_specs=tok,
        out_shape=jax.ShapeDtypeStruct(xt.shape, jnp.float32),
        compiler_params=_cparams(2),
        name="post_mixer_" + mixer,
    )(xt, *mix_args, mod, w_mix, ln_g, ln_b, ffn_w_in, ffn_w_out, ln_g, ln_b)


def _rope_tables(n_tok, head_dim):
    t = np.arange(n_tok)
    n_freq = head_dim // 4
    inv_freq = ROPE_BASE ** (-np.arange(n_freq) / n_freq)
    ang_row = (t // GRID_W)[:, None] * inv_freq[None]
    ang_col = (t % GRID_W)[:, None] * inv_freq[None]
    cos = np.concatenate([np.cos(ang_row)] * 2 + [np.cos(ang_col)] * 2, axis=1)
    sin = np.concatenate([-np.sin(ang_row), np.sin(ang_row), -np.sin(ang_col), np.sin(ang_col)], axis=1)
    reps = LANES // head_dim
    return (jnp.asarray(np.tile(cos, (1, reps)), jnp.float32),
            jnp.asarray(np.tile(sin, (1, reps)), jnp.float32))


def kernel(x, c, ctx, c_ctx, ada_w, ada_b, ln_g, ln_b, ffn_w_in, ffn_w_out, na_w_qkv, na_w_out, na_rpb,
           sc_w_in, sc_conv, sc_w_out):
    nb, s, d = x.shape
    n_ctx = ctx.shape[1]
    depth = ada_w.shape[0]
    head_dim = d // N_HEADS
    alpha = (2.0 * depth) ** 0.25
    q_scale = head_dim ** -0.5
    last_na = max(range(0, depth, N_MIXERS))
    rows = s // GRID_W
    assert s % TM_LATENT == 0 and TM_LATENT % GRID_W == 0
    assert rows >= NA_KH and rows % NA_ROW_BLOCK == 0 and rows // NA_ROW_BLOCK >= 3
    assert LANES // head_dim == HEADS_PER_SLAB

    ctx_row = nb
    n_rows = -(-(nb + 1) // SUBLANES) * SUBLANES
    cc = jnp.concatenate([c, c_ctx[None], jnp.zeros((n_rows - nb - 1, d), c.dtype)], axis=0)
    mod = _ada_modulation(cc, ada_w, ada_b)

    bf = jnp.bfloat16
    ffn_w_in, ffn_w_out = ffn_w_in.astype(bf), ffn_w_out.astype(bf)
    na_w_qkv, na_w_out = na_w_qkv.astype(bf), na_w_out.astype(bf)
    sc_w_in, sc_w_out = sc_w_in.astype(bf), sc_w_out.astype(bf)
    ln_g4, ln_b4 = ln_g.reshape(depth, 3, 1, d), ln_b.reshape(depth, 3, 1, d)
    rope_tabs = _rope_tables(s, head_dim)

    lat = dict(mod_row=None, tm=TM_LATENT)
    cx = dict(mod_row=ctx_row, tm=n_ctx)

    for i in range(depth):
        ctx_in = i <= last_na
        ctx_out = i < last_na
        j = i // N_MIXERS
        mixer = "na" if i % N_MIXERS == 0 else "conv"
        w_proj, w_mix = (na_w_qkv, na_w_out) if mixer == "na" else (sc_w_in, sc_w_out)
        pre = functools.partial(_pre_mixer, mod=mod, ln_g=ln_g4, ln_b=ln_b4, ffn_w_in=ffn_w_in,
                                ffn_w_out=ffn_w_out, w_proj=w_proj, mixer=mixer, layer=i, j=j, alpha=alpha,
                                q_scale=q_scale)
        post = functools.partial(_post_mixer, mod=mod, ln_g=ln_g4, ln_b=ln_b4, w_mix=w_mix, ffn_w_in=ffn_w_in,
                                 ffn_w_out=ffn_w_out, conv_w=sc_conv, mixer=mixer, layer=i, j=j, alpha=alpha)

        x, *x_mix = pre(x, rope_tabs=rope_tabs if mixer == "na" else None, **lat)
        if ctx_in:
            ctx, *c_mix = pre(ctx, rope_tabs=None, **cx)

        if mixer == "na":
            (q, k, v), (qc, kc, vc) = x_mix, c_mix
            x_mix, c_mix = _neighbourhood_attention(q, k, v, qc, kc, vc, _na_bias_table(na_rpb[j]),
                                                    ctx_queries=ctx_out)

        x = post(x, x_mix, **lat)
        if ctx_out:
            ctx = post(ctx, c_mix, **cx)
    return x
```

```python
import functools

import numpy as np
import jax
import jax.numpy as jnp
from jax import lax
from jax.experimental import pallas as pl
from jax.experimental.pallas import tpu as pltpu

GRID_W = 64
N_HEADS = 16
N_MIXERS = 2
NA_KH = 8
NA_KW = 16
CONV_W = 3
ROPE_BASE = 10000.0
N_MOD = 9
LN_EPS = 1e-6
MASKED = -1e30

LANES = 128
SUBLANES = 8
HEADS_PER_SLAB = 2
VMEM_LIMIT = 56 * 1024 * 1024

TM_LATENT = 512
TM_POST_NA = 2 * TM_LATENT
F_CHUNK = 256
NA_ROW_BLOCK = 4
NA_SOFTMAX_ROWS = 32


def _cparams(n_axes):
    return pltpu.CompilerParams(dimension_semantics=("arbitrary",) * n_axes,
                                vmem_limit_bytes=VMEM_LIMIT)


def _resident(block_shape, index_map):
    return pl.BlockSpec(block_shape, index_map, pipeline_mode=pl.Buffered(1))


def _layer_norm(z, g, b):
    mu = jnp.mean(z, axis=-1, keepdims=True)
    zc = z - mu
    var = jnp.mean(zc * zc, axis=-1, keepdims=True)
    return zc * lax.rsqrt(var + LN_EPS) * g + b


def _silu(a):
    return a * jax.nn.sigmoid(a)


def _modulated(x, mod_ref, k):
    return (x * (1 + mod_ref[k + 1:k + 2, :]) + mod_ref[k:k + 1, :]).astype(jnp.bfloat16)


def _swiglu_pre_norm(x, mod_ref, k0, win_ref, wout_ref, alpha):
    h = _modulated(x, mod_ref, k0)
    f = wout_ref.shape[0]
    y = jnp.zeros(x.shape, jnp.float32)
    for f0 in range(0, f, F_CHUNK):
        fc = min(F_CHUNK, f - f0)
        a = jnp.dot(h, win_ref[:, f0:f0 + fc], preferred_element_type=jnp.float32)
        u = jnp.dot(h, win_ref[:, f + f0:f + f0 + fc], preferred_element_type=jnp.float32)
        hid = (_silu(a) * u).astype(jnp.bfloat16)
        y = y + jnp.dot(hid, wout_ref[f0:f0 + fc, :], preferred_element_type=jnp.float32)
    return alpha * x + 0.5 * mod_ref[k0 + 2:k0 + 3, :] * y


def _swiglu_half_step(x, mod_ref, k0, win_ref, wout_ref, g, b, alpha):
    return _layer_norm(_swiglu_pre_norm(x, mod_ref, k0, win_ref, wout_ref, alpha), g, b)


def _ada_kernel(c_ref, w_ref, b_ref, o_ref):
    s = _silu(c_ref[...]).astype(jnp.bfloat16)
    o_ref[...] = jnp.dot(s, w_ref[...].astype(jnp.bfloat16),
                         preferred_element_type=jnp.float32) + b_ref[...]


def _ada_modulation(cc, ada_w, ada_b):
    depth, d, n = ada_w.shape
    r = cc.shape[0]
    tn = d
    out = pl.pallas_call(
        _ada_kernel,
        grid=(depth, n // tn),
        in_specs=[pl.BlockSpec((r, d), lambda i, j: (0, 0)),
                  pl.BlockSpec((None, d, tn), lambda i, j: (i, 0, j)),
                  pl.BlockSpec((None, 1, tn), lambda i, j: (i, 0, j))],
        out_specs=pl.BlockSpec((None, r, tn), lambda i, j: (i, 0, j)),
        out_shape=jax.ShapeDtypeStruct((depth, r, n), jnp.float32),
        compiler_params=_cparams(2),
        name="ada_modulation",
    )(cc, ada_w, ada_b.reshape(depth, 1, n))
    return out.reshape(depth, r, N_MOD, d)


def _tok_spec(tm, d):
    return pl.BlockSpec((None, tm, d), lambda b, t: (b, t, 0))


def _mod_spec(layer, mod_row, d):
    if mod_row is None:
        return pl.BlockSpec((None, None, N_MOD, d), lambda b, t: (layer, b, 0, 0))
    return pl.BlockSpec((None, None, N_MOD, d), lambda b, t: (layer, mod_row, 0, 0))


def _ln_spec(layer, k, d):
    return pl.BlockSpec((None, None, 1, d), lambda b, t: (layer, k, 0, 0))


def _ffn_specs(layer, which, d, f):
    return [_resident((None, None, d, 2 * f), lambda b, t: (layer, which, 0, 0)),
            _resident((None, None, f, d), lambda b, t: (layer, which, 0, 0))]


def _rope_slab(v, cos, sin, first_half):
    up = pltpu.roll(v, LANES - 16, 1)
    dn = pltpu.roll(v, 16, 1)
    return v * cos + jnp.where(first_half, up, dn) * sin


def _pre_mixer_kernel(x_ref, mod_ref, win_ref, wout_ref, g_ref, b_ref, wp_ref, *rest, mixer, rope, alpha, q_scale):
    if rope:
        cos_ref, sin_ref, x1_ref, *outs = rest
    else:
        x1_ref, *outs = rest
    x1 = _swiglu_half_step(x_ref[...], mod_ref, 0, win_ref, wout_ref, g_ref[...], b_ref[...], alpha)
    x1_ref[...] = x1
    d = x1.shape[1]
    h = _modulated(x1, mod_ref, 3)

    def proj(c0):
        return jnp.dot(h, wp_ref[:, c0:c0 + d], preferred_element_type=jnp.float32)

    if mixer == "conv":
        bg_ref, u_ref = outs
        bg_ref[...] = proj(0)
        u_ref[...] = proj(d) * proj(2 * d)
        return
    q_ref, k_ref, v_ref = outs
    vv = proj(2 * d).astype(v_ref.dtype)
    ones = jnp.ones((vv.shape[0], LANES), v_ref.dtype)
    for p in range(d // LANES):
        v_ref[:, 2 * p * LANES:(2 * p + 1) * LANES] = vv[:, p * LANES:(p + 1) * LANES]
        v_ref[:, (2 * p + 1) * LANES:(2 * p + 2) * LANES] = ones
    for out_ref, c0, mult in ((q_ref, 0, q_scale), (k_ref, d, None)):
        y = proj(c0)
        if mult is not None:
            y = y * mult
        if rope:
            cos, sin = cos_ref[...], sin_ref[...]
            lane = lax.broadcasted_iota(jnp.int32, (1, LANES), 1)
            first_half = (lane % 32) < 16
            for s0 in range(0, d, LANES):
                out_ref[:, s0:s0 + LANES] = _rope_slab(y[:, s0:s0 + LANES], cos, sin,
                                                       first_half).astype(out_ref.dtype)
        else:
            out_ref[...] = y.astype(out_ref.dtype)


def _pre_mixer(xt, mod, ln_g, ln_b, ffn_w_in, ffn_w_out, w_proj, rope_tabs, *, mixer, layer, j, mod_row, tm,
               alpha, q_scale):
    nb, s, d = xt.shape
    f = ffn_w_out.shape[2]
    rope = rope_tabs is not None
    in_specs = ([_tok_spec(tm, d), _mod_spec(layer, mod_row, d)] + _ffn_specs(layer, 0, d, f)
                + [_ln_spec(layer, 0, d), _ln_spec(layer, 0, d),
                   _resident((None, d, 3 * d), lambda b, t: (j, 0, 0))])
    args = [xt, mod, ffn_w_in, ffn_w_out, ln_g, ln_b, w_proj]
    if rope:
        in_specs += [pl.BlockSpec((tm, LANES), lambda b, t: (t, 0))] * 2
        args += list(rope_tabs)
    f32 = jax.ShapeDtypeStruct(xt.shape, jnp.float32)
    bf16 = jax.ShapeDtypeStruct(xt.shape, jnp.bfloat16)
    v_ext = jax.ShapeDtypeStruct((nb, s, 2 * d), jnp.bfloat16)
    out_shape = [f32, f32, f32] if mixer == "conv" else [f32, bf16, bf16, v_ext]
    return pl.pallas_call(
        functools.partial(_pre_mixer_kernel, mixer=mixer, rope=rope, alpha=alpha, q_scale=q_scale),
        grid=(nb, s // tm),
        in_specs=in_specs,
        out_specs=[_tok_spec(tm, o.shape[2]) for o in out_shape],
        out_shape=out_shape,
        compiler_params=_cparams(2),
        name="pre_mixer_" + mixer,
    )(*args)


def _stack_heads(q, head0):
    zero = jnp.zeros_like(q)
    return jnp.concatenate([jnp.where(head0, q, zero), jnp.where(head0, zero, q)], axis=0)


def _unstack_heads(o, head0):
    n = o.shape[0] // 2
    return jnp.where(head0, o[:n], o[n:])


def _dot_nt(a, b):
    return lax.dot_general(a, b, (((1,), (1,)), ((), ())), preferred_element_type=jnp.float32)


def _na_kernel(q_ref, k_ref, v_ref, kc_ref, vc_ref, bias_ref, *rest, rows, ctx_queries):
    if ctx_queries:
        qc_ref, o_ref, oc_ref, s_scr, p_scr = rest
    else:
        o_ref, s_scr, p_scr = rest
    head0 = lax.broadcasted_iota(jnp.int32, (1, LANES), 1) < (LANES // HEADS_PER_SLAB)
    nk = NA_KH * GRID_W
    n_stack = HEADS_PER_SLAB * GRID_W
    n_ctx = kc_ref.shape[0]

    def window_start(r):
        rs = min(max(r - NA_KH // 2, 0), rows - NA_KH)
        return rs, rs * GRID_W

    def scores(blk):
        r0 = blk * NA_ROW_BLOCK
        q_blk = q_ref[r0 * GRID_W:(r0 + NA_ROW_BLOCK) * GRID_W, :]
        zero = jnp.zeros_like(q_blk)
        q_heads = (jnp.where(head0, q_blk, zero), jnp.where(head0, zero, q_blk))
        for hh, q_head in enumerate(q_heads):
            s_ctx = _dot_nt(q_head, kc_ref[...]).reshape(NA_ROW_BLOCK, GRID_W, n_ctx)
            s_scr[r0:r0 + NA_ROW_BLOCK, hh * GRID_W:(hh + 1) * GRID_W, nk:] = s_ctx
        for i in range(NA_ROW_BLOCK):
            r = r0 + i
            rs, k0 = window_start(r)
            qs = jnp.concatenate([q_head[i * GRID_W:(i + 1) * GRID_W] for q_head in q_heads], axis=0)
            dy0 = (NA_KH - 1) - (r - rs)
            bias = jnp.concatenate([bias_ref[dy0 + 2 * m] for m in range(NA_KH // 2)], axis=1)
            s_scr[r, :, :nk] = _dot_nt(qs, k_ref[k0:k0 + nk, :]) + bias

    def softmax(blk):
        for r in range(blk * NA_ROW_BLOCK, (blk + 1) * NA_ROW_BLOCK):
            for g0 in range(0, n_stack, NA_SOFTMAX_ROWS):
                s = s_scr[r, g0:g0 + NA_SOFTMAX_ROWS, :]
                z = (s - jnp.max(s, axis=-1, keepdims=True)).astype(p_scr.dtype)
                p_scr[r, g0:g0 + NA_SOFTMAX_ROWS, :] = jnp.exp(z)

    def weighted_values(blk):
        r0 = blk * NA_ROW_BLOCK
        p_ctx = p_scr[r0:r0 + NA_ROW_BLOCK, :, nk:].reshape(NA_ROW_BLOCK * n_stack, n_ctx)
        o_ctx = jnp.dot(p_ctx, vc_ref[...], preferred_element_type=jnp.float32)
        for i in range(NA_ROW_BLOCK):
            r = r0 + i
            _, k0 = window_start(r)
            o = (jnp.dot(p_scr[r, :, :nk], v_ref[k0:k0 + nk, :], preferred_element_type=jnp.float32)
                 + o_ctx[i * n_stack:(i + 1) * n_stack])
            o = _unstack_heads(o[:, :LANES] / o[:, LANES:], head0)
            o_ref[r * GRID_W:(r + 1) * GRID_W, :] = o.astype(o_ref.dtype)

    n_blk = rows // NA_ROW_BLOCK
    for step in range(n_blk + 2):
        if step >= 2:
            weighted_values(step - 2)
        if 1 <= step <= n_blk:
            softmax(step - 1)
        if step < n_blk:
            scores(step)

    if ctx_queries:
        s = _dot_nt(_stack_heads(qc_ref[...], head0), kc_ref[...])
        p = jnp.exp((s - jnp.max(s, axis=-1, keepdims=True)).astype(jnp.bfloat16))
        o = jnp.dot(p, vc_ref[...], preferred_element_type=jnp.float32)
        oc_ref[...] = _unstack_heads(o[:, :LANES] / o[:, LANES:], head0).astype(oc_ref.dtype)


def _na_bias_table(rpb):
    h, n_dy, _ = rpb.shape
    col = np.arange(GRID_W)
    win_c0 = np.clip(col - NA_KW // 2, 0, GRID_W - NA_KW)
    allowed = (col[None, :] >= win_c0[:, None]) & (col[None, :] < win_c0[:, None] + NA_KW)
    period = 2 * GRID_W
    fill = jnp.full((h, n_dy, period - (2 * NA_KW - 1)), MASKED, rpb.dtype)
    seq = jnp.concatenate([rpb[..., NA_KW - 1:], fill, rpb[..., :NA_KW - 1]], axis=-1)
    flat = jnp.tile(seq, (1, 1, GRID_W))[..., :GRID_W * (period - 1)]
    toe = flat.reshape(h, n_dy, GRID_W, period - 1)[..., :GRID_W]
    toe = jnp.where(jnp.asarray(allowed), toe, MASKED)
    two = jnp.concatenate([toe[:, :-1], toe[:, 1:]], axis=-1)
    two = two.reshape(h // HEADS_PER_SLAB, HEADS_PER_SLAB, n_dy - 1, GRID_W, 2 * GRID_W)
    return jnp.moveaxis(two, 1, 2).reshape(h // HEADS_PER_SLAB, n_dy - 1, HEADS_PER_SLAB * GRID_W, 2 * GRID_W)


def _neighbourhood_attention(q, k, v, qc, kc, vc, bias, *, ctx_queries):
    nb, s, d = q.shape
    n_ctx = kc.shape[1]
    slabs = d // LANES
    rows = s // GRID_W
    n_stack = HEADS_PER_SLAB * GRID_W
    n_keys = NA_KH * GRID_W + n_ctx
    lat = pl.BlockSpec((None, s, LANES), lambda p, b: (b, 0, p))
    ctx = pl.BlockSpec((None, n_ctx, LANES), lambda p, b: (b, 0, p))
    lat_v = pl.BlockSpec((None, s, 2 * LANES), lambda p, b: (b, 0, p))
    ctx_v = pl.BlockSpec((None, n_ctx, 2 * LANES), lambda p, b: (b, 0, p))
    in_specs = [lat, lat, lat_v, ctx, ctx_v,
                pl.BlockSpec((None,) + bias.shape[1:], lambda p, b: (p, 0, 0, 0))]
    args = [q, k, v, kc, vc, bias]
    out_specs, out_shape = [lat], [jax.ShapeDtypeStruct(q.shape, jnp.bfloat16)]
    if ctx_queries:
        in_specs.append(ctx)
        args.append(qc)
        out_specs.append(ctx)
        out_shape.append(jax.ShapeDtypeStruct(qc.shape, jnp.bfloat16))
    outs = pl.pallas_call(
        functools.partial(_na_kernel, rows=rows, ctx_queries=ctx_queries),
        grid=(slabs, nb),
        in_specs=in_specs, out_specs=out_specs, out_shape=out_shape,
        scratch_shapes=[pltpu.VMEM((rows, n_stack, n_keys), jnp.float32),
                        pltpu.VMEM((rows, n_stack, n_keys), jnp.bfloat16)],
        compiler_params=_cparams(2),
        name="neighbourhood_attention",
    )(*args)
    return (outs[0], outs[1]) if ctx_queries else (outs[0], None)


def _post_mixer_kernel(x_ref, *rest, mixer, alpha):
    if mixer == "conv":
        bg_ref, u_ref, up_ref, un_ref, cw_ref, *rest = rest
    else:
        o_ref, *rest = rest
    mod_ref, wmix_ref, g1_ref, b1_ref, win_ref, wout_ref, g2_ref, b2_ref, out_ref = rest
    if mixer == "conv":
        t, nt = pl.program_id(1), pl.num_programs(1)
        u = u_ref[...]
        tm = u.shape[0]
        row = lax.broadcasted_iota(jnp.int32, (tm, 1), 0)
        before = jnp.where(t > 0, up_ref[SUBLANES - 1:SUBLANES, :], 0.0)
        after = jnp.where(t < nt - 1, un_ref[0:1, :], 0.0)
        u_prev = jnp.where(row == 0, before, pltpu.roll(u, 1, 0))
        u_next = jnp.where(row == tm - 1, after, pltpu.roll(u, tm - 1, 0))
        y = cw_ref[0:1, :] * u_prev + cw_ref[1:2, :] * u + cw_ref[2:3, :] * u_next
        mixed = (bg_ref[...] * y).astype(jnp.bfloat16)
    else:
        mixed = o_ref[...]
    sub = min(TM_LATENT, x_ref.shape[0])
    subs = [slice(s0, s0 + sub) for s0 in range(0, x_ref.shape[0], sub)]
    zs = [jnp.dot(mixed[rs], wmix_ref[...], preferred_element_type=jnp.float32) for rs in subs]
    z2s = []
    for rs, z in zip(subs, zs):
        x2 = _layer_norm(alpha * x_ref[rs, :] + 1.0 * mod_ref[5:6, :] * z, g1_ref[...], b1_ref[...])
        z2s.append(_swiglu_pre_norm(x2, mod_ref, 6, win_ref, wout_ref, alpha))
    for rs, z2 in zip(subs, z2s):
        out_ref[rs, :] = _layer_norm(z2, g2_ref[...], b2_ref[...])


def _post_mixer(xt, mixer_in, mod, ln_g, ln_b, w_mix, ffn_w_in, ffn_w_out, conv_w, *, mixer, layer, j, mod_row, tm,
                alpha):
    nb, s, d = xt.shape
    f = ffn_w_out.shape[2]
    tok = _tok_spec(tm, d)
    if mixer == "conv":
        bg, u = mixer_in
        per_tile = tm // SUBLANES
        last = s // SUBLANES - 1
        prev_spec = pl.BlockSpec((None, SUBLANES, d), lambda b, t: (b, jnp.maximum(t * per_tile - 1, 0), 0))
        next_spec = pl.BlockSpec((None, SUBLANES, d), lambda b, t: (b, jnp.minimum((t + 1) * per_tile, last), 0))
        mix_specs = [tok, tok, prev_spec, next_spec, pl.BlockSpec((None, CONV_W, d), lambda b, t: (j, 0, 0))]
        mix_args = [bg, u, u, u, conv_w]
    else:
        mix_specs, mix_args = [tok], [mixer_in]
    return pl.pallas_call(
        functools.partial(_post_mixer_kernel, mixer=mixer, alpha=alpha),
        grid=(nb, s // tm),
        in_specs=([tok] + mix_specs
                  + [_mod_spec(layer, mod_row, d), _resident((None, d, d), lambda b, t: (j, 0, 0)),
                     _ln_spec(layer, 1, d), _ln_spec(layer, 1, d)]
                  + _ffn_specs(layer, 1, d, f) + [_ln_spec(layer, 2, d), _ln_spec(layer, 2, d)]),
        out_specs=tok,
        out_shape=jax.ShapeDtypeStruct(xt.shape, jnp.float32),
        compiler_params=_cparams(2),
        name="post_mixer_" + mixer,
    )(xt, *mix_args, mod, w_mix, ln_g, ln_b, ffn_w_in, ffn_w_out, ln_g, ln_b)


def _rope_tables(n_tok, head_dim):
    t = np.arange(n_tok)
    n_freq = head_dim // 4
    inv_freq = ROPE_BASE ** (-np.arange(n_freq) / n_freq)
    ang_row = (t // GRID_W)[:, None] * inv_freq[None]
    ang_col = (t % GRID_W)[:, None] * inv_freq[None]
    cos = np.concatenate([np.cos(ang_row)] * 2 + [np.cos(ang_col)] * 2, axis=1)
    sin = np.concatenate([-np.sin(ang_row), np.sin(ang_row), -np.sin(ang_col), np.sin(ang_col)], axis=1)
    reps = LANES // head_dim
    return (jnp.asarray(np.tile(cos, (1, reps)), jnp.float32),
            jnp.asarray(np.tile(sin, (1, reps)), jnp.float32))


def kernel(x, c, ctx, c_ctx, ada_w, ada_b, ln_g, ln_b, ffn_w_in, ffn_w_out, na_w_qkv, na_w_out, na_rpb,
           sc_w_in, sc_conv, sc_w_out):
    nb, s, d = x.shape
    n_ctx = ctx.shape[1]
    depth = ada_w.shape[0]
    head_dim = d // N_HEADS
    alpha = (2.0 * depth) ** 0.25
    q_scale = head_dim ** -0.5
    last_na = max(range(0, depth, N_MIXERS))
    rows = s // GRID_W
    assert s % TM_LATENT == 0 and TM_LATENT % GRID_W == 0
    assert rows >= NA_KH and rows % NA_ROW_BLOCK == 0 and rows // NA_ROW_BLOCK >= 3
    assert LANES // head_dim == HEADS_PER_SLAB

    ctx_row = nb
    n_rows = -(-(nb + 1) // SUBLANES) * SUBLANES
    cc = jnp.concatenate([c, c_ctx[None], jnp.zeros((n_rows - nb - 1, d), c.dtype)], axis=0)
    mod = _ada_modulation(cc, ada_w, ada_b)

    bf = jnp.bfloat16
    ffn_w_in, ffn_w_out = ffn_w_in.astype(bf), ffn_w_out.astype(bf)
    na_w_qkv, na_w_out = na_w_qkv.astype(bf), na_w_out.astype(bf)
    sc_w_in, sc_w_out = sc_w_in.astype(bf), sc_w_out.astype(bf)
    ln_g4, ln_b4 = ln_g.reshape(depth, 3, 1, d), ln_b.reshape(depth, 3, 1, d)
    rope_tabs = _rope_tables(s, head_dim)

    lat = dict(mod_row=None, tm=TM_LATENT)
    pair = max(p for p in range(1, TM_LATENT // n_ctx + 1) if nb % p == 0) if n_ctx <= TM_LATENT else 1
    cx = dict(mod_row=ctx_row, tm=pair * n_ctx)

    def paired(a):
        return a.reshape(nb // pair, pair * n_ctx, a.shape[-1])

    def unpaired(a):
        return a.reshape(nb, n_ctx, a.shape[-1])

    for i in range(depth):
        ctx_in = i <= last_na
        ctx_out = i < last_na
        j = i // N_MIXERS
        mixer = "na" if i % N_MIXERS == 0 else "conv"
        w_proj, w_mix = (na_w_qkv, na_w_out) if mixer == "na" else (sc_w_in, sc_w_out)
        pre = functools.partial(_pre_mixer, mod=mod, ln_g=ln_g4, ln_b=ln_b4, ffn_w_in=ffn_w_in,
                                ffn_w_out=ffn_w_out, w_proj=w_proj, mixer=mixer, layer=i, j=j, alpha=alpha,
                                q_scale=q_scale)
        post = functools.partial(_post_mixer, mod=mod, ln_g=ln_g4, ln_b=ln_b4, w_mix=w_mix, ffn_w_in=ffn_w_in,
                                 ffn_w_out=ffn_w_out, conv_w=sc_conv, mixer=mixer, layer=i, j=j, alpha=alpha)

        x, *x_mix = pre(x, rope_tabs=rope_tabs if mixer == "na" else None, **lat)
        if ctx_in:
            ctx, *c_mix = [unpaired(a) for a in pre(paired(ctx), rope_tabs=None, **cx)]

        if mixer == "na":
            (q, k, v), (qc, kc, vc) = x_mix, c_mix
            x_mix, c_mix = _neighbourhood_attention(q, k, v, qc, kc, vc, _na_bias_table(na_rpb[j]),
                                                    ctx_queries=ctx_out)
            x = post(x, x_mix, mod_row=None, tm=TM_POST_NA)
            if ctx_out:
                ctx = unpaired(post(paired(ctx), paired(c_mix), **cx))
        else:
            x = post(x, x_mix, **lat)
            if ctx_out:
                ctx = post(ctx, c_mix, mod_row=ctx_row, tm=n_ctx)
    return x
```

```python
import functools

import numpy as np
import jax
import jax.numpy as jnp
from jax import lax
from jax.experimental import pallas as pl
from jax.experimental.pallas import tpu as pltpu

GRID_W = 64
N_HEADS = 16
N_MIXERS = 2
NA_KH = 8
NA_KW = 16
CONV_W = 3
ROPE_BASE = 10000.0
N_MOD = 9
LN_EPS = 1e-6
MASKED = -1e30

LANES = 128
SUBLANES = 8
HEADS_PER_SLAB = 2
VMEM_LIMIT = 56 * 1024 * 1024

TM_LATENT = 512
TM_POST_NA = 2 * TM_LATENT
F_CHUNK = 256
NA_ROW_BLOCK = 4
NA_SOFTMAX_ROWS = 32


def _cparams(n_axes):
    return pltpu.CompilerParams(dimension_semantics=("arbitrary",) * n_axes,
                                vmem_limit_bytes=VMEM_LIMIT)


def _resident(block_shape, index_map):
    return pl.BlockSpec(block_shape, index_map, pipeline_mode=pl.Buffered(1))


def _layer_norm(z, g, b):
    mu = jnp.mean(z, axis=-1, keepdims=True)
    zc = z - mu
    var = jnp.mean(zc * zc, axis=-1, keepdims=True)
    return zc * lax.rsqrt(var + LN_EPS) * g + b


def _silu(a):
    return a * jax.nn.sigmoid(a)


def _modulated(x, mod_ref, k):
    return (x * (1 + mod_ref[k + 1:k + 2, :]) + mod_ref[k:k + 1, :]).astype(jnp.bfloat16)


def _swiglu_pre_norm(x, mod_ref, k0, win_ref, wout_ref, alpha):
    h = _modulated(x, mod_ref, k0)
    f = wout_ref.shape[0]
    y = jnp.zeros(x.shape, jnp.float32)
    for f0 in range(0, f, F_CHUNK):
        fc = min(F_CHUNK, f - f0)
        a = jnp.dot(h, win_ref[:, f0:f0 + fc], preferred_element_type=jnp.float32)
        u = jnp.dot(h, win_ref[:, f + f0:f + f0 + fc], preferred_element_type=jnp.float32)
        hid = (_silu(a) * u).astype(jnp.bfloat16)
        y = y + jnp.dot(hid, wout_ref[f0:f0 + fc, :], preferred_element_type=jnp.float32)
    return alpha * x + 0.5 * mod_ref[k0 + 2:k0 + 3, :] * y


def _swiglu_half_step(x, mod_ref, k0, win_ref, wout_ref, g, b, alpha):
    return _layer_norm(_swiglu_pre_norm(x, mod_ref, k0, win_ref, wout_ref, alpha), g, b)


def _ada_kernel(c_ref, w_ref, b_ref, o_ref):
    s = _silu(c_ref[...]).astype(jnp.bfloat16)
    o_ref[...] = jnp.dot(s, w_ref[...].astype(jnp.bfloat16),
                         preferred_element_type=jnp.float32) + b_ref[...]


def _ada_modulation(cc, ada_w, ada_b):
    depth, d, n = ada_w.shape
    r = cc.shape[0]
    tn = d
    out = pl.pallas_call(
        _ada_kernel,
        grid=(depth, n // tn),
        in_specs=[pl.BlockSpec((r, d), lambda i, j: (0, 0)),
                  pl.BlockSpec((None, d, tn), lambda i, j: (i, 0, j)),
                  pl.BlockSpec((None, 1, tn), lambda i, j: (i, 0, j))],
        out_specs=pl.BlockSpec((None, r, tn), lambda i, j: (i, 0, j)),
        out_shape=jax.ShapeDtypeStruct((depth, r, n), jnp.float32),
        compiler_params=_cparams(2),
        name="ada_modulation",
    )(cc, ada_w, ada_b.reshape(depth, 1, n))
    return out.reshape(depth, r, N_MOD, d)


def _tok_spec(tm, d):
    return pl.BlockSpec((None, tm, d), lambda b, t: (b, t, 0))


def _mod_spec(layer, mod_row, d):
    if mod_row is None:
        return pl.BlockSpec((None, None, N_MOD, d), lambda b, t: (layer, b, 0, 0))
    return pl.BlockSpec((None, None, N_MOD, d), lambda b, t: (layer, mod_row, 0, 0))


def _ln_spec(layer, k, d):
    return pl.BlockSpec((None, None, 1, d), lambda b, t: (layer, k, 0, 0))


def _ffn_specs(layer, which, d, f):
    return [_resident((None, None, d, 2 * f), lambda b, t: (layer, which, 0, 0)),
            _resident((None, None, f, d), lambda b, t: (layer, which, 0, 0))]


def _rope_slab(v, cos, sin, first_half):
    up = pltpu.roll(v, LANES - 16, 1)
    dn = pltpu.roll(v, 16, 1)
    return v * cos + jnp.where(first_half, up, dn) * sin


def _pre_mixer_kernel(x_ref, mod_ref, win_ref, wout_ref, g_ref, b_ref, wp_ref, *rest, mixer, rope, alpha, q_scale):
    if rope:
        cos_ref, sin_ref, x1_ref, *outs = rest
    else:
        x1_ref, *outs = rest
    x1 = _swiglu_half_step(x_ref[...], mod_ref, 0, win_ref, wout_ref, g_ref[...], b_ref[...], alpha)
    x1_ref[...] = x1
    d = x1.shape[1]
    h = _modulated(x1, mod_ref, 3)

    def proj(c0):
        return jnp.dot(h, wp_ref[:, c0:c0 + d], preferred_element_type=jnp.float32)

    if mixer == "conv":
        bg_ref, u_ref = outs
        bg_ref[...] = proj(0)
        u_ref[...] = proj(d) * proj(2 * d)
        return
    q_ref, k_ref, v_ref = outs
    for out_ref, c0, mult in ((q_ref, 0, q_scale), (k_ref, d, None)):
        y = proj(c0)
        if mult is not None:
            y = y * mult
        if rope:
            cos, sin = cos_ref[...], sin_ref[...]
            lane = lax.broadcasted_iota(jnp.int32, (1, LANES), 1)
            first_half = (lane % 32) < 16
            for s0 in range(0, d, LANES):
                out_ref[:, s0:s0 + LANES] = _rope_slab(y[:, s0:s0 + LANES], cos, sin,
                                                       first_half).astype(out_ref.dtype)
        else:
            out_ref[...] = y.astype(out_ref.dtype)
    vv = proj(2 * d).astype(v_ref.dtype)
    ones = jnp.ones((vv.shape[0], LANES), v_ref.dtype)
    for p in range(d // LANES):
        v_ref[:, 2 * p * LANES:(2 * p + 1) * LANES] = vv[:, p * LANES:(p + 1) * LANES]
        v_ref[:, (2 * p + 1) * LANES:(2 * p + 2) * LANES] = ones


def _pre_mixer(xt, mod, ln_g, ln_b, ffn_w_in, ffn_w_out, w_proj, rope_tabs, *, mixer, layer, j, mod_row, tm,
               alpha, q_scale):
    nb, s, d = xt.shape
    f = ffn_w_out.shape[2]
    rope = rope_tabs is not None
    in_specs = ([_tok_spec(tm, d), _mod_spec(layer, mod_row, d)] + _ffn_specs(layer, 0, d, f)
                + [_ln_spec(layer, 0, d), _ln_spec(layer, 0, d),
                   _resident((None, d, 3 * d), lambda b, t: (j, 0, 0))])
    args = [xt, mod, ffn_w_in, ffn_w_out, ln_g, ln_b, w_proj]
    if rope:
        in_specs += [pl.BlockSpec((tm, LANES), lambda b, t: (t, 0))] * 2
        args += list(rope_tabs)
    f32 = jax.ShapeDtypeStruct(xt.shape, jnp.float32)
    bf16 = jax.ShapeDtypeStruct(xt.shape, jnp.bfloat16)
    v_ext = jax.ShapeDtypeStruct((nb, s, 2 * d), jnp.bfloat16)
    out_shape = [f32, f32, f32] if mixer == "conv" else [f32, bf16, bf16, v_ext]
    return pl.pallas_call(
        functools.partial(_pre_mixer_kernel, mixer=mixer, rope=rope, alpha=alpha, q_scale=q_scale),
        grid=(nb, s // tm),
        in_specs=in_specs,
        out_specs=[_tok_spec(tm, o.shape[2]) for o in out_shape],
        out_shape=out_shape,
        compiler_params=_cparams(2),
        name="pre_mixer_" + mixer,
    )(*args)


def _stack_heads(q, head0):
    zero = jnp.zeros_like(q)
    return jnp.concatenate([jnp.where(head0, q, zero), jnp.where(head0, zero, q)], axis=0)


def _unstack_heads(o, head0):
    n = o.shape[0] // 2
    return jnp.where(head0, o[:n], o[n:])


def _dot_nt(a, b):
    return lax.dot_general(a, b, (((1,), (1,)), ((), ())), preferred_element_type=jnp.float32)


def _na_kernel(q_ref, k_ref, v_ref, kc_ref, vc_ref, bias_ref, *rest, rows, ctx_queries):
    if ctx_queries:
        qc_ref, o_ref, oc_ref, s_scr, p_scr = rest
    else:
        o_ref, s_scr, p_scr = rest
    head0 = lax.broadcasted_iota(jnp.int32, (1, LANES), 1) < (LANES // HEADS_PER_SLAB)
    nk = NA_KH * GRID_W
    n_stack = HEADS_PER_SLAB * GRID_W
    n_ctx = kc_ref.shape[0]

    def window_start(r):
        rs = min(max(r - NA_KH // 2, 0), rows - NA_KH)
        return rs, rs * GRID_W

    def scores(blk):
        r0 = blk * NA_ROW_BLOCK
        q_blk = q_ref[r0 * GRID_W:(r0 + NA_ROW_BLOCK) * GRID_W, :]
        zero = jnp.zeros_like(q_blk)
        q_heads = (jnp.where(head0, q_blk, zero), jnp.where(head0, zero, q_blk))
        for hh, q_head in enumerate(q_heads):
            s_ctx = _dot_nt(q_head, kc_ref[...]).reshape(NA_ROW_BLOCK, GRID_W, n_ctx)
            s_scr[r0:r0 + NA_ROW_BLOCK, hh * GRID_W:(hh + 1) * GRID_W, nk:] = s_ctx
        for i in range(NA_ROW_BLOCK):
            r = r0 + i
            rs, k0 = window_start(r)
            qs = jnp.concatenate([q_head[i * GRID_W:(i + 1) * GRID_W] for q_head in q_heads], axis=0)
            dy0 = (NA_KH - 1) - (r - rs)
            bias = jnp.concatenate([bias_ref[dy0 + 2 * m] for m in range(NA_KH // 2)], axis=1)
            s_scr[r, :, :nk] = _dot_nt(qs, k_ref[k0:k0 + nk, :]) + bias

    def softmax(blk):
        for r in range(blk * NA_ROW_BLOCK, (blk + 1) * NA_ROW_BLOCK):
            for g0 in range(0, n_stack, NA_SOFTMAX_ROWS):
                s = s_scr[r, g0:g0 + NA_SOFTMAX_ROWS, :]
                z = (s - jnp.max(s, axis=-1, keepdims=True)).astype(p_scr.dtype)
                p_scr[r, g0:g0 + NA_SOFTMAX_ROWS, :] = jnp.exp(z)

    def weighted_values(blk):
        r0 = blk * NA_ROW_BLOCK
        p_ctx = p_scr[r0:r0 + NA_ROW_BLOCK, :, nk:].reshape(NA_ROW_BLOCK * n_stack, n_ctx)
        o_ctx = jnp.dot(p_ctx, vc_ref[...], preferred_element_type=jnp.float32)
        for i in range(NA_ROW_BLOCK):
            r = r0 + i
            _, k0 = window_start(r)
            o = (jnp.dot(p_scr[r, :, :nk], v_ref[k0:k0 + nk, :], preferred_element_type=jnp.float32)
                 + o_ctx[i * n_stack:(i + 1) * n_stack])
            o = _unstack_heads(o[:, :LANES] / o[:, LANES:], head0)
            o_ref[r * GRID_W:(r + 1) * GRID_W, :] = o.astype(o_ref.dtype)

    n_blk = rows // NA_ROW_BLOCK
    for step in range(n_blk + 2):
        if step >= 2:
            weighted_values(step - 2)
        if 1 <= step <= n_blk:
            softmax(step - 1)
        if step < n_blk:
            scores(step)

    if ctx_queries:
        s = _dot_nt(_stack_heads(qc_ref[...], head0), kc_ref[...])
        p = jnp.exp((s - jnp.max(s, axis=-1, keepdims=True)).astype(jnp.bfloat16))
        o = jnp.dot(p, vc_ref[...], preferred_element_type=jnp.float32)
        oc_ref[...] = _unstack_heads(o[:, :LANES] / o[:, LANES:], head0).astype(oc_ref.dtype)


def _na_bias_table(rpb):
    h, n_dy, _ = rpb.shape
    col = np.arange(GRID_W)
    win_c0 = np.clip(col - NA_KW // 2, 0, GRID_W - NA_KW)
    allowed = (col[None, :] >= win_c0[:, None]) & (col[None, :] < win_c0[:, None] + NA_KW)
    dx = col[None, :] - col[:, None] + NA_KW - 1
    toe = jnp.full((h, n_dy, GRID_W, GRID_W), MASKED, rpb.dtype)
    for j in range(2 * NA_KW - 1):
        toe = jnp.where(jnp.asarray((dx == j) & allowed), rpb[:, :, j, None, None], toe)
    two = jnp.concatenate([toe[:, :-1], toe[:, 1:]], axis=-1)
    two = two.reshape(h // HEADS_PER_SLAB, HEADS_PER_SLAB, n_dy - 1, GRID_W, 2 * GRID_W)
    return jnp.moveaxis(two, 1, 2).reshape(h // HEADS_PER_SLAB, n_dy - 1, HEADS_PER_SLAB * GRID_W, 2 * GRID_W)


def _neighbourhood_attention(q, k, v, qc, kc, vc, bias, *, ctx_queries):
    nb, s, d = q.shape
    n_ctx = kc.shape[1]
    slabs = d // LANES
    rows = s // GRID_W
    n_stack = HEADS_PER_SLAB * GRID_W
    n_keys = NA_KH * GRID_W + n_ctx
    lat = pl.BlockSpec((None, s, LANES), lambda p, b: (b, 0, p))
    ctx = pl.BlockSpec((None, n_ctx, LANES), lambda p, b: (b, 0, p))
    lat_v = pl.BlockSpec((None, s, 2 * LANES), lambda p, b: (b, 0, p))
    ctx_v = pl.BlockSpec((None, n_ctx, 2 * LANES), lambda p, b: (b, 0, p))
    in_specs = [lat, lat, lat_v, ctx, ctx_v,
                pl.BlockSpec((None,) + bias.shape[1:], lambda p, b: (p, 0, 0, 0))]
    args = [q, k, v, kc, vc, bias]
    out_specs, out_shape = [lat], [jax.ShapeDtypeStruct(q.shape, jnp.bfloat16)]
    if ctx_queries:
        in_specs.append(ctx)
        args.append(qc)
        out_specs.append(ctx)
        out_shape.append(jax.ShapeDtypeStruct(qc.shape, jnp.bfloat16))
    outs = pl.pallas_call(
        functools.partial(_na_kernel, rows=rows, ctx_queries=ctx_queries),
        grid=(slabs, nb),
        in_specs=in_specs, out_specs=out_specs, out_shape=out_shape,
        scratch_shapes=[pltpu.VMEM((rows, n_stack, n_keys), jnp.float32),
                        pltpu.VMEM((rows, n_stack, n_keys), jnp.bfloat16)],
        compiler_params=_cparams(2),
        name="neighbourhood_attention",
    )(*args)
    return (outs[0], outs[1]) if ctx_queries else (outs[0], None)


def _post_mixer_kernel(x_ref, *rest, mixer, alpha, seq_len):
    if mixer == "conv":
        bg_ref, u_ref, up_ref, un_ref, cw_ref, *rest = rest
    else:
        o_ref, *rest = rest
    mod_ref, wmix_ref, g1_ref, b1_ref, win_ref, wout_ref, g2_ref, b2_ref, out_ref = rest
    if mixer == "conv":
        t = pl.program_id(1)
        u = u_ref[...]
        tm = u.shape[0]
        row = lax.broadcasted_iota(jnp.int32, (tm, 1), 0)
        if seq_len >= tm:
            per_seq = seq_len // tm
            first = (row == 0) & (t % per_seq == 0)
            last = (row == tm - 1) & (t % per_seq == per_seq - 1)
        else:
            first = functools.reduce(jnp.logical_or, [row == r for r in range(0, tm, seq_len)])
            last = functools.reduce(jnp.logical_or, [row == r + seq_len - 1 for r in range(0, tm, seq_len)])
        u_prev = jnp.where(row == 0, up_ref[SUBLANES - 1:SUBLANES, :], pltpu.roll(u, 1, 0))
        u_next = jnp.where(row == tm - 1, un_ref[0:1, :], pltpu.roll(u, tm - 1, 0))
        u_prev, u_next = jnp.where(first, 0.0, u_prev), jnp.where(last, 0.0, u_next)
        y = cw_ref[0:1, :] * u_prev + cw_ref[1:2, :] * u + cw_ref[2:3, :] * u_next
        mixed = (bg_ref[...] * y).astype(jnp.bfloat16)
    else:
        mixed = o_ref[...]
    sub = min(TM_LATENT, x_ref.shape[0])
    subs = [slice(s0, s0 + sub) for s0 in range(0, x_ref.shape[0], sub)]
    zs = [jnp.dot(mixed[rs], wmix_ref[...], preferred_element_type=jnp.float32) for rs in subs]
    z2s = []
    for rs, z in zip(subs, zs):
        x2 = _layer_norm(alpha * x_ref[rs, :] + 1.0 * mod_ref[5:6, :] * z, g1_ref[...], b1_ref[...])
        z2s.append(_swiglu_pre_norm(x2, mod_ref, 6, win_ref, wout_ref, alpha))
    for rs, z2 in zip(subs, z2s):
        out_ref[rs, :] = _layer_norm(z2, g2_ref[...], b2_ref[...])


def _post_mixer(xt, mixer_in, mod, ln_g, ln_b, w_mix, ffn_w_in, ffn_w_out, conv_w, *, mixer, layer, j, mod_row, tm,
                alpha, seq_len=None):
    nb, s, d = xt.shape
    f = ffn_w_out.shape[2]
    tok = _tok_spec(tm, d)
    if mixer == "conv":
        bg, u = mixer_in
        per_tile = tm // SUBLANES
        last = s // SUBLANES - 1
        prev_spec = pl.BlockSpec((None, SUBLANES, d), lambda b, t: (b, jnp.maximum(t * per_tile - 1, 0), 0))
        next_spec = pl.BlockSpec((None, SUBLANES, d), lambda b, t: (b, jnp.minimum((t + 1) * per_tile, last), 0))
        mix_specs = [tok, tok, prev_spec, next_spec, pl.BlockSpec((None, CONV_W, d), lambda b, t: (j, 0, 0))]
        mix_args = [bg, u, u, u, conv_w]
    else:
        mix_specs, mix_args = [tok], [mixer_in]
    return pl.pallas_call(
        functools.partial(_post_mixer_kernel, mixer=mixer, alpha=alpha, seq_len=seq_len or s),
        grid=(nb, s // tm),
        in_specs=([tok] + mix_specs
                  + [_mod_spec(layer, mod_row, d), _resident((None, d, d), lambda b, t: (j, 0, 0)),
                     _ln_spec(layer, 1, d), _ln_spec(layer, 1, d)]
                  + _ffn_specs(layer, 1, d, f) + [_ln_spec(layer, 2, d), _ln_spec(layer, 2, d)]),
        out_specs=tok,
        out_shape=jax.ShapeDtypeStruct(xt.shape, jnp.float32),
        compiler_params=_cparams(2),
        name="post_mixer_" + mixer,
    )(xt, *mix_args, mod, w_mix, ln_g, ln_b, ffn_w_in, ffn_w_out, ln_g, ln_b)


def _rope_tables(n_tok, head_dim):
    t = np.arange(n_tok)
    n_freq = head_dim // 4
    inv_freq = ROPE_BASE ** (-np.arange(n_freq) / n_freq)
    ang_row = (t // GRID_W)[:, None] * inv_freq[None]
    ang_col = (t % GRID_W)[:, None] * inv_freq[None]
    cos = np.concatenate([np.cos(ang_row)] * 2 + [np.cos(ang_col)] * 2, axis=1)
    sin = np.concatenate([-np.sin(ang_row), np.sin(ang_row), -np.sin(ang_col), np.sin(ang_col)], axis=1)
    reps = LANES // head_dim
    return (jnp.asarray(np.tile(cos, (1, reps)), jnp.float32),
            jnp.asarray(np.tile(sin, (1, reps)), jnp.float32))


def kernel(x, c, ctx, c_ctx, ada_w, ada_b, ln_g, ln_b, ffn_w_in, ffn_w_out, na_w_qkv, na_w_out, na_rpb,
           sc_w_in, sc_conv, sc_w_out):
    nb, s, d = x.shape
    n_ctx = ctx.shape[1]
    depth = ada_w.shape[0]
    head_dim = d // N_HEADS
    alpha = (2.0 * depth) ** 0.25
    q_scale = head_dim ** -0.5
    last_na = max(range(0, depth, N_MIXERS))
    rows = s // GRID_W
    assert s % TM_LATENT == 0 and TM_LATENT % GRID_W == 0
    assert rows >= NA_KH and rows % NA_ROW_BLOCK == 0 and rows // NA_ROW_BLOCK >= 3
    assert LANES // head_dim == HEADS_PER_SLAB

    ctx_row = nb
    n_rows = -(-(nb + 1) // SUBLANES) * SUBLANES
    cc = jnp.concatenate([c, c_ctx[None], jnp.zeros((n_rows - nb - 1, d), c.dtype)], axis=0)
    mod = _ada_modulation(cc, ada_w, ada_b)

    bf = jnp.bfloat16
    ffn_w_in, ffn_w_out = ffn_w_in.astype(bf), ffn_w_out.astype(bf)
    na_w_qkv, na_w_out = na_w_qkv.astype(bf), na_w_out.astype(bf)
    sc_w_in, sc_w_out = sc_w_in.astype(bf), sc_w_out.astype(bf)
    ln_g4, ln_b4 = ln_g.reshape(depth, 3, 1, d), ln_b.reshape(depth, 3, 1, d)
    rope_tabs = _rope_tables(s, head_dim)

    lat = dict(mod_row=None, tm=TM_LATENT)
    pair = max(p for p in range(1, TM_LATENT // n_ctx + 1) if nb % p == 0) if n_ctx <= TM_LATENT else 1
    cx = dict(mod_row=ctx_row, tm=pair * n_ctx)

    def paired(a):
        return a.reshape(nb // pair, pair * n_ctx, a.shape[-1])

    def unpaired(a):
        return a.reshape(nb, n_ctx, a.shape[-1])

    for i in range(depth):
        ctx_in = i <= last_na
        ctx_out = i < last_na
        j = i // N_MIXERS
        mixer = "na" if i % N_MIXERS == 0 else "conv"
        w_proj, w_mix = (na_w_qkv, na_w_out) if mixer == "na" else (sc_w_in, sc_w_out)
        pre = functools.partial(_pre_mixer, mod=mod, ln_g=ln_g4, ln_b=ln_b4, ffn_w_in=ffn_w_in,
                                ffn_w_out=ffn_w_out, w_proj=w_proj, mixer=mixer, layer=i, j=j, alpha=alpha,
                                q_scale=q_scale)
        post = functools.partial(_post_mixer, mod=mod, ln_g=ln_g4, ln_b=ln_b4, w_mix=w_mix, ffn_w_in=ffn_w_in,
                                 ffn_w_out=ffn_w_out, conv_w=sc_conv, mixer=mixer, layer=i, j=j, alpha=alpha)

        x, *x_mix = pre(x, rope_tabs=rope_tabs if mixer == "na" else None, **lat)
        if ctx_in:
            ctx, *c_mix = [unpaired(a) for a in pre(paired(ctx), rope_tabs=None, **cx)]

        if mixer == "na":
            (q, k, v), (qc, kc, vc) = x_mix, c_mix
            x_mix, c_mix = _neighbourhood_attention(q, k, v, qc, kc, vc, _na_bias_table(na_rpb[j]),
                                                    ctx_queries=ctx_out)
            c_mix = paired(c_mix) if ctx_out else None
        else:
            c_mix = [paired(a) for a in c_mix] if ctx_out else None
        x = post(x, x_mix, mod_row=None, tm=TM_POST_NA if mixer == "na" else TM_LATENT)
        if ctx_out:
            ctx = unpaired(post(paired(ctx), c_mix, seq_len=n_ctx, **cx))
    return x
```

```python
import functools

import numpy as np
import jax
import jax.numpy as jnp
from jax import lax
from jax.experimental import pallas as pl
from jax.experimental.pallas import tpu as pltpu

GRID_W = 64
N_HEADS = 16
N_MIXERS = 2
NA_KH = 8
NA_KW = 16
CONV_W = 3
ROPE_BASE = 10000.0
N_MOD = 9
LN_EPS = 1e-6
MASKED = -1e30

LANES = 128
SUBLANES = 8
HEADS_PER_SLAB = 2
VMEM_LIMIT = 56 * 1024 * 1024

TM_LATENT = 512
TM_POST_NA = 2 * TM_LATENT
F_CHUNK = 256
NA_ROW_BLOCK = 4
NA_SOFTMAX_ROWS = 32


def _cparams(n_axes):
    return pltpu.CompilerParams(dimension_semantics=("arbitrary",) * n_axes,
                                vmem_limit_bytes=VMEM_LIMIT)


def _resident(block_shape, index_map):
    return pl.BlockSpec(block_shape, index_map, pipeline_mode=pl.Buffered(1))


def _layer_norm(z, g, b):
    mu = jnp.mean(z, axis=-1, keepdims=True)
    zc = z - mu
    var = jnp.mean(zc * zc, axis=-1, keepdims=True)
    return zc * lax.rsqrt(var + LN_EPS) * g + b


def _silu(a):
    return a * jax.nn.sigmoid(a)


def _modulated(x, mod_ref, k):
    return (x * (1 + mod_ref[k + 1:k + 2, :]) + mod_ref[k:k + 1, :]).astype(jnp.bfloat16)


def _swiglu_pre_norm(x, mod_ref, k0, win_ref, wout_ref, alpha):
    h = _modulated(x, mod_ref, k0)
    f = wout_ref.shape[0]
    y = jnp.zeros(x.shape, jnp.float32)
    for f0 in range(0, f, F_CHUNK):
        fc = min(F_CHUNK, f - f0)
        a = jnp.dot(h, win_ref[:, f0:f0 + fc], preferred_element_type=jnp.float32)
        u = jnp.dot(h, win_ref[:, f + f0:f + f0 + fc], preferred_element_type=jnp.float32)
        hid = (_silu(a) * u).astype(jnp.bfloat16)
        y = y + jnp.dot(hid, wout_ref[f0:f0 + fc, :], preferred_element_type=jnp.float32)
    return alpha * x + 0.5 * mod_ref[k0 + 2:k0 + 3, :] * y


def _swiglu_half_step(x, mod_ref, k0, win_ref, wout_ref, g, b, alpha):
    return _layer_norm(_swiglu_pre_norm(x, mod_ref, k0, win_ref, wout_ref, alpha), g, b)


def _ada_kernel(c_ref, w_ref, b_ref, o_ref):
    s = _silu(c_ref[...]).astype(jnp.bfloat16)
    o_ref[...] = jnp.dot(s, w_ref[...].astype(jnp.bfloat16),
                         preferred_element_type=jnp.float32) + b_ref[...]


def _ada_modulation(cc, ada_w, ada_b):
    depth, d, n = ada_w.shape
    r = cc.shape[0]
    tn = d
    out = pl.pallas_call(
        _ada_kernel,
        grid=(depth, n // tn),
        in_specs=[pl.BlockSpec((r, d), lambda i, j: (0, 0)),
                  pl.BlockSpec((None, d, tn), lambda i, j: (i, 0, j)),
                  pl.BlockSpec((None, 1, tn), lambda i, j: (i, 0, j))],
        out_specs=pl.BlockSpec((None, r, tn), lambda i, j: (i, 0, j)),
        out_shape=jax.ShapeDtypeStruct((depth, r, n), jnp.float32),
        compiler_params=_cparams(2),
        name="ada_modulation",
    )(cc, ada_w, ada_b.reshape(depth, 1, n))
    return out.reshape(depth, r, N_MOD, d)


def _tok_spec(tm, d):
    return pl.BlockSpec((None, tm, d), lambda b, t: (b, t, 0))


def _mod_spec(layer, mod_row, d):
    if mod_row is None:
        return pl.BlockSpec((None, None, N_MOD, d), lambda b, t: (layer, b, 0, 0))
    return pl.BlockSpec((None, None, N_MOD, d), lambda b, t: (layer, mod_row, 0, 0))


def _ln_spec(layer, k, d):
    return pl.BlockSpec((None, None, 1, d), lambda b, t: (layer, k, 0, 0))


def _ffn_specs(layer, which, d, f):
    return [_resident((None, None, d, 2 * f), lambda b, t: (layer, which, 0, 0)),
            _resident((None, None, f, d), lambda b, t: (layer, which, 0, 0))]


def _rope_slab(v, cos, sin, first_half):
    up = pltpu.roll(v, LANES - 16, 1)
    dn = pltpu.roll(v, 16, 1)
    return v * cos + jnp.where(first_half, up, dn) * sin


def _pre_mixer_kernel(x_ref, mod_ref, win_ref, wout_ref, g_ref, b_ref, wp_ref, *rest, mixer, rope, alpha, q_scale):
    if rope:
        cos_ref, sin_ref, x1_ref, *outs = rest
    else:
        x1_ref, *outs = rest
    x1 = _swiglu_half_step(x_ref[...], mod_ref, 0, win_ref, wout_ref, g_ref[...], b_ref[...], alpha)
    x1_ref[...] = x1
    d = x1.shape[1]
    h = _modulated(x1, mod_ref, 3)

    def proj(c0):
        return jnp.dot(h, wp_ref[:, c0:c0 + d], preferred_element_type=jnp.float32)

    if mixer == "conv":
        bg_ref, u_ref = outs
        bg_ref[...] = proj(0)
        u_ref[...] = proj(d) * proj(2 * d)
        return
    q_ref, k_ref, v_ref = outs
    for out_ref, c0, mult in ((q_ref, 0, q_scale), (k_ref, d, None)):
        y = proj(c0)
        if mult is not None:
            y = y * mult
        if rope:
            cos, sin = cos_ref[...], sin_ref[...]
            lane = lax.broadcasted_iota(jnp.int32, (1, LANES), 1)
            first_half = (lane % 32) < 16
            for s0 in range(0, d, LANES):
                out_ref[:, s0:s0 + LANES] = _rope_slab(y[:, s0:s0 + LANES], cos, sin,
                                                       first_half).astype(out_ref.dtype)
        else:
            out_ref[...] = y.astype(out_ref.dtype)
    vv = proj(2 * d).astype(v_ref.dtype)
    ones = jnp.ones((vv.shape[0], LANES), v_ref.dtype)
    for p in range(d // LANES):
        v_ref[:, 2 * p * LANES:(2 * p + 1) * LANES] = vv[:, p * LANES:(p + 1) * LANES]
        v_ref[:, (2 * p + 1) * LANES:(2 * p + 2) * LANES] = ones


def _pre_mixer(xt, mod, ln_g, ln_b, ffn_w_in, ffn_w_out, w_proj, rope_tabs, *, mixer, layer, j, mod_row, tm,
               alpha, q_scale):
    nb, s, d = xt.shape
    f = ffn_w_out.shape[2]
    rope = rope_tabs is not None
    in_specs = ([_tok_spec(tm, d), _mod_spec(layer, mod_row, d)] + _ffn_specs(layer, 0, d, f)
                + [_ln_spec(layer, 0, d), _ln_spec(layer, 0, d),
                   _resident((None, d, 3 * d), lambda b, t: (j, 0, 0))])
    args = [xt, mod, ffn_w_in, ffn_w_out, ln_g, ln_b, w_proj]
    if rope:
        in_specs += [pl.BlockSpec((tm, LANES), lambda b, t: (t, 0))] * 2
        args += list(rope_tabs)
    f32 = jax.ShapeDtypeStruct(xt.shape, jnp.float32)
    bf16 = jax.ShapeDtypeStruct(xt.shape, jnp.bfloat16)
    v_ext = jax.ShapeDtypeStruct((nb, s, 2 * d), jnp.bfloat16)
    out_shape = [f32, f32, f32] if mixer == "conv" else [f32, bf16, bf16, v_ext]
    return pl.pallas_call(
        functools.partial(_pre_mixer_kernel, mixer=mixer, rope=rope, alpha=alpha, q_scale=q_scale),
        grid=(nb, s // tm),
        in_specs=in_specs,
        out_specs=[_tok_spec(tm, o.shape[2]) for o in out_shape],
        out_shape=out_shape,
        compiler_params=_cparams(2),
        name="pre_mixer_" + mixer,
    )(*args)


def _stack_heads(q, head0):
    zero = jnp.zeros_like(q)
    return jnp.concatenate([jnp.where(head0, q, zero), jnp.where(head0, zero, q)], axis=0)


def _unstack_heads(o, head0):
    n = o.shape[0] // 2
    return jnp.where(head0, o[:n], o[n:])


def _dot_nt(a, b):
    return lax.dot_general(a, b, (((1,), (1,)), ((), ())), preferred_element_type=jnp.float32)


def _na_kernel(q_ref, k_ref, v_ref, kc_ref, vc_ref, bias_ref, *rest, rows, ctx_queries):
    if ctx_queries:
        qc_ref, o_ref, oc_ref, s_scr, p_scr = rest
    else:
        o_ref, s_scr, p_scr = rest
    head0 = lax.broadcasted_iota(jnp.int32, (1, LANES), 1) < (LANES // HEADS_PER_SLAB)
    nk = NA_KH * GRID_W
    n_stack = HEADS_PER_SLAB * GRID_W
    n_ctx = kc_ref.shape[0]

    def window_start(r):
        rs = min(max(r - NA_KH // 2, 0), rows - NA_KH)
        return rs, rs * GRID_W

    def scores(blk):
        r0 = blk * NA_ROW_BLOCK
        q_blk = q_ref[r0 * GRID_W:(r0 + NA_ROW_BLOCK) * GRID_W, :]
        zero = jnp.zeros_like(q_blk)
        q_heads = (jnp.where(head0, q_blk, zero), jnp.where(head0, zero, q_blk))
        for hh, q_head in enumerate(q_heads):
            s_ctx = _dot_nt(q_head, kc_ref[...]).reshape(NA_ROW_BLOCK, GRID_W, n_ctx)
            s_scr[r0:r0 + NA_ROW_BLOCK, hh * GRID_W:(hh + 1) * GRID_W, nk:] = s_ctx
        for i in range(NA_ROW_BLOCK):
            r = r0 + i
            rs, k0 = window_start(r)
            qs = jnp.concatenate([q_head[i * GRID_W:(i + 1) * GRID_W] for q_head in q_heads], axis=0)
            dy0 = (NA_KH - 1) - (r - rs)
            bias = jnp.concatenate([bias_ref[dy0 + 2 * m] for m in range(NA_KH // 2)], axis=1)
            s_scr[r, :, :nk] = _dot_nt(qs, k_ref[k0:k0 + nk, :]) + bias

    def softmax(blk):
        for r in range(blk * NA_ROW_BLOCK, (blk + 1) * NA_ROW_BLOCK):
            for g0 in range(0, n_stack, NA_SOFTMAX_ROWS):
                s = s_scr[r, g0:g0 + NA_SOFTMAX_ROWS, :]
                z = (s - jnp.max(s, axis=-1, keepdims=True)).astype(p_scr.dtype)
                p_scr[r, g0:g0 + NA_SOFTMAX_ROWS, :] = jnp.exp(z)

    def weighted_values(blk):
        r0 = blk * NA_ROW_BLOCK
        p_ctx = p_scr[r0:r0 + NA_ROW_BLOCK, :, nk:].reshape(NA_ROW_BLOCK * n_stack, n_ctx)
        o_ctx = jnp.dot(p_ctx, vc_ref[...], preferred_element_type=jnp.float32)
        for i in range(NA_ROW_BLOCK):
            r = r0 + i
            _, k0 = window_start(r)
            o = (jnp.dot(p_scr[r, :, :nk], v_ref[k0:k0 + nk, :], preferred_element_type=jnp.float32)
                 + o_ctx[i * n_stack:(i + 1) * n_stack])
            o = _unstack_heads(o[:, :LANES] / o[:, LANES:], head0)
            o_ref[r * GRID_W:(r + 1) * GRID_W, :] = o.astype(o_ref.dtype)

    n_blk = rows // NA_ROW_BLOCK
    for step in range(n_blk + 2):
        if step >= 2:
            weighted_values(step - 2)
        if 1 <= step <= n_blk:
            softmax(step - 1)
        if step < n_blk:
            scores(step)

    if ctx_queries:
        s = _dot_nt(_stack_heads(qc_ref[...], head0), kc_ref[...])
        p = jnp.exp((s - jnp.max(s, axis=-1, keepdims=True)).astype(jnp.bfloat16))
        o = jnp.dot(p, vc_ref[...], preferred_element_type=jnp.float32)
        oc_ref[...] = _unstack_heads(o[:, :LANES] / o[:, LANES:], head0).astype(oc_ref.dtype)


def _na_bias_table(rpb):
    h, n_dy, n_dx = rpb.shape
    n_slab = h // HEADS_PER_SLAB
    col = np.arange(GRID_W)
    win_c0 = np.clip(col - NA_KW // 2, 0, GRID_W - NA_KW)
    allowed = (col[None, :] >= win_c0[:, None]) & (col[None, :] < win_c0[:, None] + NA_KW)
    dx = col[None, :] - col[:, None] + NA_KW - 1
    one_hot = ((dx[None] == np.arange(n_dx)[:, None, None]) & allowed[None]).astype(np.float32)
    pairs = jnp.stack([rpb[:, :-1], rpb[:, 1:]], axis=2).reshape(n_slab, HEADS_PER_SLAB, n_dy - 1, 2, n_dx)
    tbl = jnp.einsum("phyej,jqk->pyhqek", pairs, one_hot, precision=lax.Precision.HIGHEST)
    tbl = tbl + np.where(allowed, 0.0, MASKED).astype(np.float32)[:, None, :]
    return tbl.reshape(n_slab, n_dy - 1, HEADS_PER_SLAB * GRID_W, 2 * GRID_W)


def _neighbourhood_attention(q, k, v, qc, kc, vc, bias, *, ctx_queries):
    nb, s, d = q.shape
    n_ctx = kc.shape[1]
    slabs = d // LANES
    rows = s // GRID_W
    n_stack = HEADS_PER_SLAB * GRID_W
    n_keys = NA_KH * GRID_W + n_ctx
    lat = pl.BlockSpec((None, s, LANES), lambda p, b: (b, 0, p))
    ctx = pl.BlockSpec((None, n_ctx, LANES), lambda p, b: (b, 0, p))
    lat_v = pl.BlockSpec((None, s, 2 * LANES), lambda p, b: (b, 0, p))
    ctx_v = pl.BlockSpec((None, n_ctx, 2 * LANES), lambda p, b: (b, 0, p))
    in_specs = [lat, lat, lat_v, ctx, ctx_v,
                pl.BlockSpec((None,) + bias.shape[1:], lambda p, b: (p, 0, 0, 0))]
    args = [q, k, v, kc, vc, bias]
    out_specs, out_shape = [lat], [jax.ShapeDtypeStruct(q.shape, jnp.bfloat16)]
    if ctx_queries:
        in_specs.append(ctx)
        args.append(qc)
        out_specs.append(ctx)
        out_shape.append(jax.ShapeDtypeStruct(qc.shape, jnp.bfloat16))
    outs = pl.pallas_call(
        functools.partial(_na_kernel, rows=rows, ctx_queries=ctx_queries),
        grid=(slabs, nb),
        in_specs=in_specs, out_specs=out_specs, out_shape=out_shape,
        scratch_shapes=[pltpu.VMEM((rows, n_stack, n_keys), jnp.float32),
                        pltpu.VMEM((rows, n_stack, n_keys), jnp.bfloat16)],
        compiler_params=_cparams(2),
        name="neighbourhood_attention",
    )(*args)
    return (outs[0], outs[1]) if ctx_queries else (outs[0], None)


def _post_mixer_kernel(x_ref, *rest, mixer, alpha, seq_len):
    if mixer == "conv":
        bg_ref, u_ref, up_ref, un_ref, cw_ref, *rest = rest
    else:
        o_ref, *rest = rest
    mod_ref, wmix_ref, g1_ref, b1_ref, win_ref, wout_ref, g2_ref, b2_ref, out_ref = rest
    if mixer == "conv":
        t = pl.program_id(1)
        u = u_ref[...]
        tm = u.shape[0]
        row = lax.broadcasted_iota(jnp.int32, (tm, 1), 0)
        if seq_len >= tm:
            per_seq = seq_len // tm
            first = (row == 0) & (t % per_seq == 0)
            last = (row == tm - 1) & (t % per_seq == per_seq - 1)
        else:
            first = functools.reduce(jnp.logical_or, [row == r for r in range(0, tm, seq_len)])
            last = functools.reduce(jnp.logical_or, [row == r + seq_len - 1 for r in range(0, tm, seq_len)])
        u_prev = jnp.where(row == 0, up_ref[SUBLANES - 1:SUBLANES, :], pltpu.roll(u, 1, 0))
        u_next = jnp.where(row == tm - 1, un_ref[0:1, :], pltpu.roll(u, tm - 1, 0))
        u_prev, u_next = jnp.where(first, 0.0, u_prev), jnp.where(last, 0.0, u_next)
        y = cw_ref[0:1, :] * u_prev + cw_ref[1:2, :] * u + cw_ref[2:3, :] * u_next
        mixed = (bg_ref[...] * y).astype(jnp.bfloat16)
    else:
        mixed = o_ref[...]
    sub = min(TM_LATENT, x_ref.shape[0])
    subs = [slice(s0, s0 + sub) for s0 in range(0, x_ref.shape[0], sub)]
    zs = [jnp.dot(mixed[rs], wmix_ref[...], preferred_element_type=jnp.float32) for rs in subs]
    z2s = []
    for rs, z in zip(subs, zs):
        x2 = _layer_norm(alpha * x_ref[rs, :] + 1.0 * mod_ref[5:6, :] * z, g1_ref[...], b1_ref[...])
        z2s.append(_swiglu_pre_norm(x2, mod_ref, 6, win_ref, wout_ref, alpha))
    for rs, z2 in zip(subs, z2s):
        out_ref[rs, :] = _layer_norm(z2, g2_ref[...], b2_ref[...])


def _post_mixer(xt, mixer_in, mod, ln_g, ln_b, w_mix, ffn_w_in, ffn_w_out, conv_w, *, mixer, layer, j, mod_row, tm,
                alpha, seq_len=None):
    nb, s, d = xt.shape
    f = ffn_w_out.shape[2]
    tok = _tok_spec(tm, d)
    if mixer == "conv":
        bg, u = mixer_in
        per_tile = tm // SUBLANES
        last = s // SUBLANES - 1
        prev_spec = pl.BlockSpec((None, SUBLANES, d), lambda b, t: (b, jnp.maximum(t * per_tile - 1, 0), 0))
        next_spec = pl.BlockSpec((None, SUBLANES, d), lambda b, t: (b, jnp.minimum((t + 1) * per_tile, last), 0))
        mix_specs = [tok, tok, prev_spec, next_spec, pl.BlockSpec((None, CONV_W, d), lambda b, t: (j, 0, 0))]
        mix_args = [bg, u, u, u, conv_w]
    else:
        mix_specs, mix_args = [tok], [mixer_in]
    return pl.pallas_call(
        functools.partial(_post_mixer_kernel, mixer=mixer, alpha=alpha, seq_len=seq_len or s),
        grid=(nb, s // tm),
        in_specs=([tok] + mix_specs
                  + [_mod_spec(layer, mod_row, d), _resident((None, d, d), lambda b, t: (j, 0, 0)),
                     _ln_spec(layer, 1, d), _ln_spec(layer, 1, d)]
                  + _ffn_specs(layer, 1, d, f) + [_ln_spec(layer, 2, d), _ln_spec(layer, 2, d)]),
        out_specs=tok,
        out_shape=jax.ShapeDtypeStruct(xt.shape, jnp.float32),
        compiler_params=_cparams(2),
        name="post_mixer_" + mixer,
    )(xt, *mix_args, mod, w_mix, ln_g, ln_b, ffn_w_in, ffn_w_out, ln_g, ln_b)


def _rope_tables(n_tok, head_dim):
    t = np.arange(n_tok)
    n_freq = head_dim // 4
    inv_freq = ROPE_BASE ** (-np.arange(n_freq) / n_freq)
    ang_row = (t // GRID_W)[:, None] * inv_freq[None]
    ang_col = (t % GRID_W)[:, None] * inv_freq[None]
    cos = np.concatenate([np.cos(ang_row)] * 2 + [np.cos(ang_col)] * 2, axis=1)
    sin = np.concatenate([-np.sin(ang_row), np.sin(ang_row), -np.sin(ang_col), np.sin(ang_col)], axis=1)
    reps = LANES // head_dim
    return (jnp.asarray(np.tile(cos, (1, reps)), jnp.float32),
            jnp.asarray(np.tile(sin, (1, reps)), jnp.float32))


def kernel(x, c, ctx, c_ctx, ada_w, ada_b, ln_g, ln_b, ffn_w_in, ffn_w_out, na_w_qkv, na_w_out, na_rpb,
           sc_w_in, sc_conv, sc_w_out):
    nb, s, d = x.shape
    n_ctx = ctx.shape[1]
    depth = ada_w.shape[0]
    head_dim = d // N_HEADS
    alpha = (2.0 * depth) ** 0.25
    q_scale = head_dim ** -0.5
    last_na = max(range(0, depth, N_MIXERS))
    rows = s // GRID_W
    assert s % TM_LATENT == 0 and TM_LATENT % GRID_W == 0
    assert rows >= NA_KH and rows % NA_ROW_BLOCK == 0 and rows // NA_ROW_BLOCK >= 3
    assert LANES // head_dim == HEADS_PER_SLAB

    ctx_row = nb
    n_rows = -(-(nb + 1) // SUBLANES) * SUBLANES
    cc = jnp.concatenate([c, c_ctx[None], jnp.zeros((n_rows - nb - 1, d), c.dtype)], axis=0)
    mod = _ada_modulation(cc, ada_w, ada_b)

    bf = jnp.bfloat16
    ffn_w_in, ffn_w_out = ffn_w_in.astype(bf), ffn_w_out.astype(bf)
    na_w_qkv, na_w_out = na_w_qkv.astype(bf), na_w_out.astype(bf)
    sc_w_in, sc_w_out = sc_w_in.astype(bf), sc_w_out.astype(bf)
    ln_g4, ln_b4 = ln_g.reshape(depth, 3, 1, d), ln_b.reshape(depth, 3, 1, d)
    rope_tabs = _rope_tables(s, head_dim)

    lat = dict(mod_row=None, tm=TM_LATENT)
    pair = max(p for p in range(1, TM_LATENT // n_ctx + 1) if nb % p == 0) if n_ctx <= TM_LATENT else 1
    cx = dict(mod_row=ctx_row, tm=pair * n_ctx)

    def paired(a):
        return a.reshape(nb // pair, pair * n_ctx, a.shape[-1])

    def unpaired(a):
        return a.reshape(nb, n_ctx, a.shape[-1])

    for i in range(depth):
        ctx_in = i <= last_na
        ctx_out = i < last_na
        j = i // N_MIXERS
        mixer = "na" if i % N_MIXERS == 0 else "conv"
        w_proj, w_mix = (na_w_qkv, na_w_out) if mixer == "na" else (sc_w_in, sc_w_out)
        pre = functools.partial(_pre_mixer, mod=mod, ln_g=ln_g4, ln_b=ln_b4, ffn_w_in=ffn_w_in,
                                ffn_w_out=ffn_w_out, w_proj=w_proj, mixer=mixer, layer=i, j=j, alpha=alpha,
                                q_scale=q_scale)
        post = functools.partial(_post_mixer, mod=mod, ln_g=ln_g4, ln_b=ln_b4, w_mix=w_mix, ffn_w_in=ffn_w_in,
                                 ffn_w_out=ffn_w_out, conv_w=sc_conv, mixer=mixer, layer=i, j=j, alpha=alpha)

        x, *x_mix = pre(x, rope_tabs=rope_tabs if mixer == "na" else None, **lat)
        if ctx_in:
            ctx, *c_mix = [unpaired(a) for a in pre(paired(ctx), rope_tabs=None, **cx)]

        if mixer == "na":
            (q, k, v), (qc, kc, vc) = x_mix, c_mix
            x_mix, c_mix = _neighbourhood_attention(q, k, v, qc, kc, vc, _na_bias_table(na_rpb[j]),
                                                    ctx_queries=ctx_out)
            c_mix = paired(c_mix) if ctx_out else None
        else:
            c_mix = [paired(a) for a in c_mix] if ctx_out else None
        x = post(x, x_mix, mod_row=None, tm=TM_POST_NA if mixer == "na" else TM_LATENT)
        if ctx_out:
            ctx = unpaired(post(paired(ctx), c_mix, seq_len=n_ctx, **cx))
    return x
```

```python
import functools

import numpy as np
import jax
import jax.numpy as jnp
from jax import lax
from jax.experimental import pallas as pl
from jax.experimental.pallas import tpu as pltpu

GRID_W = 64
N_HEADS = 16
N_MIXERS = 2
NA_KH = 8
NA_KW = 16
CONV_W = 3
ROPE_BASE = 10000.0
N_MOD = 9
LN_EPS = 1e-6
MASKED = -1e30

LANES = 128
SUBLANES = 8
HEADS_PER_SLAB = 2
VMEM_LIMIT = 56 * 1024 * 1024

TM_LATENT = 512
TM_POST_NA = 2 * TM_LATENT
F_CHUNK = 256
NA_ROW_BLOCK = 4
NA_SOFTMAX_ROWS = 32
NA_BATCH = 4
NA_RING = 3


def _cparams(n_axes):
    return pltpu.CompilerParams(dimension_semantics=("arbitrary",) * n_axes,
                                vmem_limit_bytes=VMEM_LIMIT)


def _resident(block_shape, index_map):
    return pl.BlockSpec(block_shape, index_map, pipeline_mode=pl.Buffered(1))


def _layer_norm(z, g, b):
    mu = jnp.mean(z, axis=-1, keepdims=True)
    zc = z - mu
    var = jnp.mean(zc * zc, axis=-1, keepdims=True)
    return zc * lax.rsqrt(var + LN_EPS) * g + b


def _silu(a):
    return a * jax.nn.sigmoid(a)


def _modulated(x, mod_ref, k):
    return (x * (1 + mod_ref[k + 1:k + 2, :]) + mod_ref[k:k + 1, :]).astype(jnp.bfloat16)


def _swiglu_pre_norm(x, mod_ref, k0, win_ref, wout_ref, alpha):
    h = _modulated(x, mod_ref, k0)
    f = wout_ref.shape[0]
    y = jnp.zeros(x.shape, jnp.float32)
    for f0 in range(0, f, F_CHUNK):
        fc = min(F_CHUNK, f - f0)
        a = jnp.dot(h, win_ref[:, f0:f0 + fc], preferred_element_type=jnp.float32)
        u = jnp.dot(h, win_ref[:, f + f0:f + f0 + fc], preferred_element_type=jnp.float32)
        hid = (_silu(a) * u).astype(jnp.bfloat16)
        y = y + jnp.dot(hid, wout_ref[f0:f0 + fc, :], preferred_element_type=jnp.float32)
    return alpha * x + 0.5 * mod_ref[k0 + 2:k0 + 3, :] * y


def _swiglu_half_step(x, mod_ref, k0, win_ref, wout_ref, g, b, alpha):
    return _layer_norm(_swiglu_pre_norm(x, mod_ref, k0, win_ref, wout_ref, alpha), g, b)


def _ada_kernel(c_ref, w_ref, b_ref, o_ref):
    s = _silu(c_ref[...]).astype(jnp.bfloat16)
    o_ref[...] = jnp.dot(s, w_ref[...].astype(jnp.bfloat16),
                         preferred_element_type=jnp.float32) + b_ref[...]


def _ada_modulation(cc, ada_w, ada_b):
    depth, d, n = ada_w.shape
    r = cc.shape[0]
    tn = d
    out = pl.pallas_call(
        _ada_kernel,
        grid=(depth, n // tn),
        in_specs=[pl.BlockSpec((r, d), lambda i, j: (0, 0)),
                  pl.BlockSpec((None, d, tn), lambda i, j: (i, 0, j)),
                  pl.BlockSpec((None, 1, tn), lambda i, j: (i, 0, j))],
        out_specs=pl.BlockSpec((None, r, tn), lambda i, j: (i, 0, j)),
        out_shape=jax.ShapeDtypeStruct((depth, r, n), jnp.float32),
        compiler_params=_cparams(2),
        name="ada_modulation",
    )(cc, ada_w, ada_b.reshape(depth, 1, n))
    return out.reshape(depth, r, N_MOD, d)


def _tok_spec(tm, d):
    return pl.BlockSpec((None, tm, d), lambda b, t: (b, t, 0))


def _mod_spec(layer, mod_row, d):
    if mod_row is None:
        return pl.BlockSpec((None, None, N_MOD, d), lambda b, t: (layer, b, 0, 0))
    return pl.BlockSpec((None, None, N_MOD, d), lambda b, t: (layer, mod_row, 0, 0))


def _ln_spec(layer, k, d):
    return pl.BlockSpec((None, None, 1, d), lambda b, t: (layer, k, 0, 0))


def _ffn_specs(layer, which, d, f):
    return [_resident((None, None, d, 2 * f), lambda b, t: (layer, which, 0, 0)),
            _resident((None, None, f, d), lambda b, t: (layer, which, 0, 0))]


def _rope_slab(v, cos, sin, first_half):
    up = pltpu.roll(v, LANES - 16, 1)
    dn = pltpu.roll(v, 16, 1)
    return v * cos + jnp.where(first_half, up, dn) * sin


def _pre_mixer_kernel(x_ref, mod_ref, win_ref, wout_ref, g_ref, b_ref, wp_ref, *rest, mixer, rope, alpha, q_scale):
    if rope:
        cos_ref, sin_ref, x1_ref, *outs = rest
    else:
        x1_ref, *outs = rest
    x1 = _swiglu_half_step(x_ref[...], mod_ref, 0, win_ref, wout_ref, g_ref[...], b_ref[...], alpha)
    x1_ref[...] = x1
    d = x1.shape[1]
    h = _modulated(x1, mod_ref, 3)

    def proj(c0):
        return jnp.dot(h, wp_ref[:, c0:c0 + d], preferred_element_type=jnp.float32)

    if mixer == "conv":
        bg_ref, u_ref = outs
        bg_ref[...] = proj(0)
        u_ref[...] = proj(d) * proj(2 * d)
        return
    q_ref, k_ref, v_ref = outs
    for out_ref, c0, mult in ((q_ref, 0, q_scale), (k_ref, d, None)):
        y = proj(c0)
        if mult is not None:
            y = y * mult
        if rope:
            cos, sin = cos_ref[...], sin_ref[...]
            lane = lax.broadcasted_iota(jnp.int32, (1, LANES), 1)
            first_half = (lane % 32) < 16
            for s0 in range(0, d, LANES):
                out_ref[:, s0:s0 + LANES] = _rope_slab(y[:, s0:s0 + LANES], cos, sin,
                                                       first_half).astype(out_ref.dtype)
        else:
            out_ref[...] = y.astype(out_ref.dtype)
    vv = proj(2 * d).astype(v_ref.dtype)
    ones = jnp.ones((vv.shape[0], LANES), v_ref.dtype)
    for p in range(d // LANES):
        v_ref[:, 2 * p * LANES:(2 * p + 1) * LANES] = vv[:, p * LANES:(p + 1) * LANES]
        v_ref[:, (2 * p + 1) * LANES:(2 * p + 2) * LANES] = ones


def _pre_mixer(xt, mod, ln_g, ln_b, ffn_w_in, ffn_w_out, w_proj, rope_tabs, *, mixer, layer, j, mod_row, tm,
               alpha, q_scale):
    nb, s, d = xt.shape
    f = ffn_w_out.shape[2]
    rope = rope_tabs is not None
    in_specs = ([_tok_spec(tm, d), _mod_spec(layer, mod_row, d)] + _ffn_specs(layer, 0, d, f)
                + [_ln_spec(layer, 0, d), _ln_spec(layer, 0, d),
                   _resident((None, d, 3 * d), lambda b, t: (j, 0, 0))])
    args = [xt, mod, ffn_w_in, ffn_w_out, ln_g, ln_b, w_proj]
    if rope:
        in_specs += [pl.BlockSpec((tm, LANES), lambda b, t: (t, 0))] * 2
        args += list(rope_tabs)
    f32 = jax.ShapeDtypeStruct(xt.shape, jnp.float32)
    bf16 = jax.ShapeDtypeStruct(xt.shape, jnp.bfloat16)
    v_ext = jax.ShapeDtypeStruct((nb, s, 2 * d), jnp.bfloat16)
    out_shape = [f32, f32, f32] if mixer == "conv" else [f32, bf16, bf16, v_ext]
    return pl.pallas_call(
        functools.partial(_pre_mixer_kernel, mixer=mixer, rope=rope, alpha=alpha, q_scale=q_scale),
        grid=(nb, s // tm),
        in_specs=in_specs,
        out_specs=[_tok_spec(tm, o.shape[2]) for o in out_shape],
        out_shape=out_shape,
        compiler_params=_cparams(2),
        name="pre_mixer_" + mixer,
    )(*args)


def _stack_heads(q, head0):
    zero = jnp.zeros_like(q)
    return jnp.concatenate([jnp.where(head0, q, zero), jnp.where(head0, zero, q)], axis=0)


def _unstack_heads(o, head0):
    n = o.shape[0] // 2
    return jnp.where(head0, o[:n], o[n:])


def _dot_nt(a, b):
    return lax.dot_general(a, b, (((1,), (1,)), ((), ())), preferred_element_type=jnp.float32)


def _na_kernel(q_ref, k_ref, v_ref, kc_ref, vc_ref, bias_ref, *rest, rows, ctx_queries):
    if ctx_queries:
        qc_ref, o_ref, oc_ref, s_scr, p_scr = rest
    else:
        o_ref, s_scr, p_scr = rest
    head0 = lax.broadcasted_iota(jnp.int32, (1, LANES), 1) < (LANES // HEADS_PER_SLAB)
    nk = NA_KH * GRID_W
    n_stack = HEADS_PER_SLAB * GRID_W
    n_batch, n_ctx = kc_ref.shape[0], kc_ref.shape[1]

    def window_start(r):
        rs = min(max(r - NA_KH // 2, 0), rows - NA_KH)
        return rs, rs * GRID_W

    n_blk = rows // NA_ROW_BLOCK

    def scores(g):
        bb, r0, slot = g // n_blk, (g % n_blk) * NA_ROW_BLOCK, g % NA_RING
        q_blk = q_ref[bb, r0 * GRID_W:(r0 + NA_ROW_BLOCK) * GRID_W, :]
        zero = jnp.zeros_like(q_blk)
        q_heads = (jnp.where(head0, q_blk, zero), jnp.where(head0, zero, q_blk))
        for hh, q_head in enumerate(q_heads):
            s_ctx = _dot_nt(q_head, kc_ref[bb]).reshape(NA_ROW_BLOCK, GRID_W, n_ctx)
            s_scr[slot, :, hh * GRID_W:(hh + 1) * GRID_W, nk:] = s_ctx
        for i in range(NA_ROW_BLOCK):
            r = r0 + i
            rs, k0 = window_start(r)
            qs = jnp.concatenate([q_head[i * GRID_W:(i + 1) * GRID_W] for q_head in q_heads], axis=0)
            dy0 = (NA_KH - 1) - (r - rs)
            bias = jnp.concatenate([bias_ref[dy0 + 2 * m] for m in range(NA_KH // 2)], axis=1)
            s_scr[slot, i, :, :nk] = _dot_nt(qs, k_ref[bb, k0:k0 + nk, :]) + bias

    def softmax(g):
        slot = g % NA_RING
        for i in range(NA_ROW_BLOCK):
            for g0 in range(0, n_stack, NA_SOFTMAX_ROWS):
                s = s_scr[slot, i, g0:g0 + NA_SOFTMAX_ROWS, :]
                z = (s - jnp.max(s, axis=-1, keepdims=True)).astype(p_scr.dtype)
                p_scr[slot, i, g0:g0 + NA_SOFTMAX_ROWS, :] = jnp.exp(z)

    def weighted_values(g):
        bb, r0, slot = g // n_blk, (g % n_blk) * NA_ROW_BLOCK, g % NA_RING
        p_ctx = p_scr[slot, :, :, nk:].reshape(NA_ROW_BLOCK * n_stack, n_ctx)
        o_ctx = jnp.dot(p_ctx, vc_ref[bb], preferred_element_type=jnp.float32)
        for i in range(NA_ROW_BLOCK):
            r = r0 + i
            _, k0 = window_start(r)
            o = (jnp.dot(p_scr[slot, i, :, :nk], v_ref[bb, k0:k0 + nk, :], preferred_element_type=jnp.float32)
                 + o_ctx[i * n_stack:(i + 1) * n_stack])
            o = _unstack_heads(o[:, :LANES] / o[:, LANES:], head0)
            o_ref[bb, r * GRID_W:(r + 1) * GRID_W, :] = o.astype(o_ref.dtype)

    n_all = n_batch * n_blk
    for step in range(n_all + 2):
        if step >= 2:
            weighted_values(step - 2)
        if 1 <= step <= n_all:
            softmax(step - 1)
        if step < n_all:
            scores(step)

    if ctx_queries:
        for bb in range(n_batch):
            s = _dot_nt(_stack_heads(qc_ref[bb], head0), kc_ref[bb])
            p = jnp.exp((s - jnp.max(s, axis=-1, keepdims=True)).astype(jnp.bfloat16))
            o = jnp.dot(p, vc_ref[bb], preferred_element_type=jnp.float32)
            oc_ref[bb] = _unstack_heads(o[:, :LANES] / o[:, LANES:], head0).astype(oc_ref.dtype)


def _na_bias_table(rpb):
    h, n_dy, n_dx = rpb.shape
    n_slab = h // HEADS_PER_SLAB
    col = np.arange(GRID_W)
    win_c0 = np.clip(col - NA_KW // 2, 0, GRID_W - NA_KW)
    allowed = (col[None, :] >= win_c0[:, None]) & (col[None, :] < win_c0[:, None] + NA_KW)
    row, lane = jnp.arange(HEADS_PER_SLAB * GRID_W), jnp.arange(2 * GRID_W)
    c_hh, c_e, c_j = jnp.unravel_index(jnp.arange(HEADS_PER_SLAB * 2 * n_dx), (HEADS_PER_SLAB, 2, n_dx))
    q, k = row % GRID_W, lane % GRID_W
    ok = jnp.asarray(allowed)[q[:, None], k[None, :]]
    one_hot = ((c_hh[:, None, None] == (row // GRID_W)[None, :, None])
               & (c_e[:, None, None] == (lane // GRID_W)[None, None, :])
               & (c_j[:, None, None] == (k[None, :] - q[:, None] + NA_KW - 1)[None])
               & ok[None]).astype(rpb.dtype)
    pairs = jnp.stack([rpb[:, :-1], rpb[:, 1:]], axis=2)
    pairs = pairs.reshape(n_slab, HEADS_PER_SLAB, n_dy - 1, 2 * n_dx)
    pairs = jnp.moveaxis(pairs, 1, 2).reshape(n_slab, n_dy - 1, HEADS_PER_SLAB * 2 * n_dx)
    tbl = jnp.einsum("pyc,cab->pyab", pairs, one_hot, precision=lax.Precision.HIGHEST)
    return tbl + jnp.where(ok, 0.0, MASKED).astype(rpb.dtype)


def _neighbourhood_attention(q, k, v, qc, kc, vc, bias, *, ctx_queries):
    nb, s, d = q.shape
    n_ctx = kc.shape[1]
    slabs = d // LANES
    rows = s // GRID_W
    n_stack = HEADS_PER_SLAB * GRID_W
    n_keys = NA_KH * GRID_W + n_ctx
    n_batch = max(n for n in range(1, NA_BATCH + 1) if nb % n == 0)
    lat = pl.BlockSpec((n_batch, s, LANES), lambda p, b: (b, 0, p))
    ctx = pl.BlockSpec((n_batch, n_ctx, LANES), lambda p, b: (b, 0, p))
    lat_v = pl.BlockSpec((n_batch, s, 2 * LANES), lambda p, b: (b, 0, p))
    ctx_v = pl.BlockSpec((n_batch, n_ctx, 2 * LANES), lambda p, b: (b, 0, p))
    in_specs = [lat, lat, lat_v, ctx, ctx_v,
                pl.BlockSpec((None,) + bias.shape[1:], lambda p, b: (p, 0, 0, 0))]
    args = [q, k, v, kc, vc, bias]
    out_specs, out_shape = [lat], [jax.ShapeDtypeStruct(q.shape, jnp.bfloat16)]
    if ctx_queries:
        in_specs.append(ctx)
        args.append(qc)
        out_specs.append(ctx)
        out_shape.append(jax.ShapeDtypeStruct(qc.shape, jnp.bfloat16))
    outs = pl.pallas_call(
        functools.partial(_na_kernel, rows=rows, ctx_queries=ctx_queries),
        grid=(slabs, nb // n_batch),
        in_specs=in_specs, out_specs=out_specs, out_shape=out_shape,
        scratch_shapes=[pltpu.VMEM((NA_RING, NA_ROW_BLOCK, n_stack, n_keys), jnp.float32),
                        pltpu.VMEM((NA_RING, NA_ROW_BLOCK, n_stack, n_keys), jnp.bfloat16)],
        compiler_params=_cparams(2),
        name="neighbourhood_attention",
    )(*args)
    return (outs[0], outs[1]) if ctx_queries else (outs[0], None)


def _post_mixer_kernel(x_ref, *rest, mixer, alpha, seq_len):
    if mixer == "conv":
        bg_ref, u_ref, up_ref, un_ref, cw_ref, *rest = rest
    else:
        o_ref, *rest = rest
    mod_ref, wmix_ref, g1_ref, b1_ref, win_ref, wout_ref, g2_ref, b2_ref, out_ref = rest
    if mixer == "conv":
        t = pl.program_id(1)
        u = u_ref[...]
        tm = u.shape[0]
        row = lax.broadcasted_iota(jnp.int32, (tm, 1), 0)
        if seq_len >= tm:
            per_seq = seq_len // tm
            first = (row == 0) & (t % per_seq == 0)
            last = (row == tm - 1) & (t % per_seq == per_seq - 1)
        else:
            first = functools.reduce(jnp.logical_or, [row == r for r in range(0, tm, seq_len)])
            last = functools.reduce(jnp.logical_or, [row == r + seq_len - 1 for r in range(0, tm, seq_len)])
        u_prev = jnp.where(row == 0, up_ref[SUBLANES - 1:SUBLANES, :], pltpu.roll(u, 1, 0))
        u_next = jnp.where(row == tm - 1, un_ref[0:1, :], pltpu.roll(u, tm - 1, 0))
        u_prev, u_next = jnp.where(first, 0.0, u_prev), jnp.where(last, 0.0, u_next)
        y = cw_ref[0:1, :] * u_prev + cw_ref[1:2, :] * u + cw_ref[2:3, :] * u_next
        mixed = (bg_ref[...] * y).astype(jnp.bfloat16)
    else:
        mixed = o_ref[...]
    sub = min(TM_LATENT, x_ref.shape[0])
    subs = [slice(s0, s0 + sub) for s0 in range(0, x_ref.shape[0], sub)]
    zs = [jnp.dot(mixed[rs], wmix_ref[...], preferred_element_type=jnp.float32) for rs in subs]
    z2s = []
    for rs, z in zip(subs, zs):
        x2 = _layer_norm(alpha * x_ref[rs, :] + 1.0 * mod_ref[5:6, :] * z, g1_ref[...], b1_ref[...])
        z2s.append(_swiglu_pre_norm(x2, mod_ref, 6, win_ref, wout_ref, alpha))
    for rs, z2 in zip(subs, z2s):
        out_ref[rs, :] = _layer_norm(z2, g2_ref[...], b2_ref[...])


def _post_mixer(xt, mixer_in, mod, ln_g, ln_b, w_mix, ffn_w_in, ffn_w_out, conv_w, *, mixer, layer, j, mod_row, tm,
                alpha, seq_len=None):
    nb, s, d = xt.shape
    f = ffn_w_out.shape[2]
    tok = _tok_spec(tm, d)
    if mixer == "conv":
        bg, u = mixer_in
        per_tile = tm // SUBLANES
        last = s // SUBLANES - 1
        prev_spec = pl.BlockSpec((None, SUBLANES, d), lambda b, t: (b, jnp.maximum(t * per_tile - 1, 0), 0))
        next_spec = pl.BlockSpec((None, SUBLANES, d), lambda b, t: (b, jnp.minimum((t + 1) * per_tile, last), 0))
        mix_specs = [tok, tok, prev_spec, next_spec, pl.BlockSpec((None, CONV_W, d), lambda b, t: (j, 0, 0))]
        mix_args = [bg, u, u, u, conv_w]
    else:
        mix_specs, mix_args = [tok], [mixer_in]
    return pl.pallas_call(
        functools.partial(_post_mixer_kernel, mixer=mixer, alpha=alpha, seq_len=seq_len or s),
        grid=(nb, s // tm),
        in_specs=([tok] + mix_specs
                  + [_mod_spec(layer, mod_row, d), _resident((None, d, d), lambda b, t: (j, 0, 0)),
                     _ln_spec(layer, 1, d), _ln_spec(layer, 1, d)]
                  + _ffn_specs(layer, 1, d, f) + [_ln_spec(layer, 2, d), _ln_spec(layer, 2, d)]),
        out_specs=tok,
        out_shape=jax.ShapeDtypeStruct(xt.shape, jnp.float32),
        compiler_params=_cparams(2),
        name="post_mixer_" + mixer,
    )(xt, *mix_args, mod, w_mix, ln_g, ln_b, ffn_w_in, ffn_w_out, ln_g, ln_b)


def _rope_tables(n_tok, head_dim):
    t = np.arange(n_tok)
    n_freq = head_dim // 4
    inv_freq = ROPE_BASE ** (-np.arange(n_freq) / n_freq)
    ang_row = (t // GRID_W)[:, None] * inv_freq[None]
    ang_col = (t % GRID_W)[:, None] * inv_freq[None]
    cos = np.concatenate([np.cos(ang_row)] * 2 + [np.cos(ang_col)] * 2, axis=1)
    sin = np.concatenate([-np.sin(ang_row), np.sin(ang_row), -np.sin(ang_col), np.sin(ang_col)], axis=1)
    reps = LANES // head_dim
    return (jnp.asarray(np.tile(cos, (1, reps)), jnp.float32),
            jnp.asarray(np.tile(sin, (1, reps)), jnp.float32))


def kernel(x, c, ctx, c_ctx, ada_w, ada_b, ln_g, ln_b, ffn_w_in, ffn_w_out, na_w_qkv, na_w_out, na_rpb,
           sc_w_in, sc_conv, sc_w_out):
    nb, s, d = x.shape
    n_ctx = ctx.shape[1]
    depth = ada_w.shape[0]
    head_dim = d // N_HEADS
    alpha = (2.0 * depth) ** 0.25
    q_scale = head_dim ** -0.5
    last_na = max(range(0, depth, N_MIXERS))
    rows = s // GRID_W
    assert s % TM_LATENT == 0 and TM_LATENT % GRID_W == 0
    assert rows >= NA_KH and rows % NA_ROW_BLOCK == 0 and rows // NA_ROW_BLOCK >= 3
    assert LANES // head_dim == HEADS_PER_SLAB

    ctx_row = nb
    n_rows = -(-(nb + 1) // SUBLANES) * SUBLANES
    cc = jnp.concatenate([c, c_ctx[None], jnp.zeros((n_rows - nb - 1, d), c.dtype)], axis=0)
    mod = _ada_modulation(cc, ada_w, ada_b)

    bf = jnp.bfloat16
    ffn_w_in, ffn_w_out = ffn_w_in.astype(bf), ffn_w_out.astype(bf)
    na_w_qkv, na_w_out = na_w_qkv.astype(bf), na_w_out.astype(bf)
    sc_w_in, sc_w_out = sc_w_in.astype(bf), sc_w_out.astype(bf)
    ln_g4, ln_b4 = ln_g.reshape(depth, 3, 1, d), ln_b.reshape(depth, 3, 1, d)
    rope_tabs = _rope_tables(s, head_dim)

    lat = dict(mod_row=None, tm=TM_LATENT)
    pair = max(p for p in range(1, TM_LATENT // n_ctx + 1) if nb % p == 0) if n_ctx <= TM_LATENT else 1
    cx = dict(mod_row=ctx_row, tm=pair * n_ctx)

    def paired(a):
        return a.reshape(nb // pair, pair * n_ctx, a.shape[-1])

    def unpaired(a):
        return a.reshape(nb, n_ctx, a.shape[-1])

    for i in range(depth):
        ctx_in = i <= last_na
        ctx_out = i < last_na
        j = i // N_MIXERS
        mixer = "na" if i % N_MIXERS == 0 else "conv"
        w_proj, w_mix = (na_w_qkv, na_w_out) if mixer == "na" else (sc_w_in, sc_w_out)
        pre = functools.partial(_pre_mixer, mod=mod, ln_g=ln_g4, ln_b=ln_b4, ffn_w_in=ffn_w_in,
                                ffn_w_out=ffn_w_out, w_proj=w_proj, mixer=mixer, layer=i, j=j, alpha=alpha,
                                q_scale=q_scale)
        post = functools.partial(_post_mixer, mod=mod, ln_g=ln_g4, ln_b=ln_b4, w_mix=w_mix, ffn_w_in=ffn_w_in,
                                 ffn_w_out=ffn_w_out, conv_w=sc_conv, mixer=mixer, layer=i, j=j, alpha=alpha)

        x, *x_mix = pre(x, rope_tabs=rope_tabs if mixer == "na" else None, **lat)
        if ctx_in:
            ctx, *c_mix = [unpaired(a) for a in pre(paired(ctx), rope_tabs=None, **cx)]

        if mixer == "na":
            (q, k, v), (qc, kc, vc) = x_mix, c_mix
            x_mix, c_mix = _neighbourhood_attention(q, k, v, qc, kc, vc, _na_bias_table(na_rpb[j]),
                                                    ctx_queries=ctx_out)
            c_mix = paired(c_mix) if ctx_out else None
        else:
            c_mix = [paired(a) for a in c_mix] if ctx_out else None
        x = post(x, x_mix, mod_row=None, tm=TM_POST_NA if mixer == "na" else TM_LATENT)
        if ctx_out:
            ctx = unpaired(post(paired(ctx), c_mix, seq_len=n_ctx, **cx))
    return x
```

```python
import functools

import numpy as np
import jax
import jax.numpy as jnp
from jax import lax
from jax.experimental import pallas as pl
from jax.experimental.pallas import tpu as pltpu

GRID_W = 64
N_HEADS = 16
N_MIXERS = 2
NA_KH = 8
NA_KW = 16
CONV_W = 3
ROPE_BASE = 10000.0
N_MOD = 9
LN_EPS = 1e-6
MASKED = -1e30

LANES = 128
SUBLANES = 8
HEADS_PER_SLAB = 2
VMEM_LIMIT = 56 * 1024 * 1024

TM_LATENT = 512
TM_POST_NA = 2 * TM_LATENT
F_CHUNK = 256
NA_ROW_BLOCK = 4
NA_SOFTMAX_ROWS = 32
NA_BATCH = 4
NA_RING = 3


def _cparams(n_axes):
    return pltpu.CompilerParams(dimension_semantics=("arbitrary",) * n_axes,
                                vmem_limit_bytes=VMEM_LIMIT)


def _resident(block_shape, index_map):
    return pl.BlockSpec(block_shape, index_map, pipeline_mode=pl.Buffered(1))


def _layer_norm(z, g, b):
    mu = jnp.mean(z, axis=-1, keepdims=True)
    zc = z - mu
    var = jnp.mean(zc * zc, axis=-1, keepdims=True)
    return zc * lax.rsqrt(var + LN_EPS) * g + b


def _silu(a):
    return a * jax.nn.sigmoid(a)


def _modulated(x, mod_ref, k):
    return (x * (1 + mod_ref[k + 1:k + 2, :]) + mod_ref[k:k + 1, :]).astype(jnp.bfloat16)


def _swiglu_pre_norm(x, mod_ref, k0, win_ref, wout_ref, alpha):
    h = _modulated(x, mod_ref, k0)
    f = wout_ref.shape[0]
    y = jnp.zeros(x.shape, jnp.float32)
    for f0 in range(0, f, F_CHUNK):
        fc = min(F_CHUNK, f - f0)
        a = jnp.dot(h, win_ref[:, f0:f0 + fc], preferred_element_type=jnp.float32)
        u = jnp.dot(h, win_ref[:, f + f0:f + f0 + fc], preferred_element_type=jnp.float32)
        hid = (_silu(a) * u).astype(jnp.bfloat16)
        y = y + jnp.dot(hid, wout_ref[f0:f0 + fc, :], preferred_element_type=jnp.float32)
    return alpha * x + 0.5 * mod_ref[k0 + 2:k0 + 3, :] * y


def _swiglu_half_step(x, mod_ref, k0, win_ref, wout_ref, g, b, alpha):
    return _layer_norm(_swiglu_pre_norm(x, mod_ref, k0, win_ref, wout_ref, alpha), g, b)


def _ada_kernel(c_ref, w_ref, b_ref, o_ref):
    s = _silu(c_ref[...]).astype(jnp.bfloat16)
    o_ref[...] = jnp.dot(s, w_ref[...].astype(jnp.bfloat16),
                         preferred_element_type=jnp.float32) + b_ref[...]


def _ada_modulation(cc, ada_w, ada_b):
    depth, d, n = ada_w.shape
    r = cc.shape[0]
    tn = d
    out = pl.pallas_call(
        _ada_kernel,
        grid=(depth, n // tn),
        in_specs=[pl.BlockSpec((r, d), lambda i, j: (0, 0)),
                  pl.BlockSpec((None, d, tn), lambda i, j: (i, 0, j)),
                  pl.BlockSpec((None, 1, tn), lambda i, j: (i, 0, j))],
        out_specs=pl.BlockSpec((None, r, tn), lambda i, j: (i, 0, j)),
        out_shape=jax.ShapeDtypeStruct((depth, r, n), jnp.float32),
        compiler_params=_cparams(2),
        name="ada_modulation",
    )(cc, ada_w, ada_b.reshape(depth, 1, n))
    return out.reshape(depth, r, N_MOD, d)


def _tok_spec(tm, d):
    return pl.BlockSpec((None, tm, d), lambda b, t: (b, t, 0))


def _mod_spec(layer, mod_row, d):
    if mod_row is None:
        return pl.BlockSpec((None, None, N_MOD, d), lambda b, t: (layer, b, 0, 0))
    return pl.BlockSpec((None, None, N_MOD, d), lambda b, t: (layer, mod_row, 0, 0))


def _ln_spec(layer, k, d):
    return pl.BlockSpec((None, None, 1, d), lambda b, t: (layer, k, 0, 0))


def _ffn_specs(layer, which, d, f):
    return [_resident((None, None, d, 2 * f), lambda b, t: (layer, which, 0, 0)),
            _resident((None, None, f, d), lambda b, t: (layer, which, 0, 0))]


def _rope_slab(v, cos, sin, first_half):
    up = pltpu.roll(v, LANES - 16, 1)
    dn = pltpu.roll(v, 16, 1)
    return v * cos + jnp.where(first_half, up, dn) * sin


def _pre_mixer_kernel(x_ref, mod_ref, win_ref, wout_ref, g_ref, b_ref, wp_ref, *rest, mixer, rope, alpha, q_scale):
    if rope:
        cos_ref, sin_ref, x1_ref, *outs = rest
    else:
        x1_ref, *outs = rest
    x1 = _swiglu_half_step(x_ref[...], mod_ref, 0, win_ref, wout_ref, g_ref[...], b_ref[...], alpha)
    x1_ref[...] = x1
    d = x1.shape[1]
    h = _modulated(x1, mod_ref, 3)

    def proj(c0):
        return jnp.dot(h, wp_ref[:, c0:c0 + d], preferred_element_type=jnp.float32)

    if mixer == "conv":
        bg_ref, u_ref = outs
        bg_ref[...] = proj(0)
        u_ref[...] = proj(d) * proj(2 * d)
        return
    q_ref, k_ref, v_ref = outs
    for out_ref, c0, mult in ((q_ref, 0, q_scale), (k_ref, d, None)):
        y = proj(c0)
        if mult is not None:
            y = y * mult
        if rope:
            cos, sin = cos_ref[...], sin_ref[...]
            lane = lax.broadcasted_iota(jnp.int32, (1, LANES), 1)
            first_half = (lane % 32) < 16
            for s0 in range(0, d, LANES):
                out_ref[:, s0:s0 + LANES] = _rope_slab(y[:, s0:s0 + LANES], cos, sin,
                                                       first_half).astype(out_ref.dtype)
        else:
            out_ref[...] = y.astype(out_ref.dtype)
    vv = proj(2 * d).astype(v_ref.dtype)
    ones = jnp.ones((vv.shape[0], LANES), v_ref.dtype)
    for p in range(d // LANES):
        v_ref[:, 2 * p * LANES:(2 * p + 1) * LANES] = vv[:, p * LANES:(p + 1) * LANES]
        v_ref[:, (2 * p + 1) * LANES:(2 * p + 2) * LANES] = ones


def _pre_mixer(xt, mod, ln_g, ln_b, ffn_w_in, ffn_w_out, w_proj, rope_tabs, *, mixer, layer, j, mod_row, tm,
               alpha, q_scale):
    nb, s, d = xt.shape
    f = ffn_w_out.shape[2]
    rope = rope_tabs is not None
    in_specs = ([_tok_spec(tm, d), _mod_spec(layer, mod_row, d)] + _ffn_specs(layer, 0, d, f)
                + [_ln_spec(layer, 0, d), _ln_spec(layer, 0, d),
                   _resident((None, d, 3 * d), lambda b, t: (j, 0, 0))])
    args = [xt, mod, ffn_w_in, ffn_w_out, ln_g, ln_b, w_proj]
    if rope:
        in_specs += [pl.BlockSpec((tm, LANES), lambda b, t: (t, 0))] * 2
        args += list(rope_tabs)
    f32 = jax.ShapeDtypeStruct(xt.shape, jnp.float32)
    bf16 = jax.ShapeDtypeStruct(xt.shape, jnp.bfloat16)
    v_ext = jax.ShapeDtypeStruct((nb, s, 2 * d), jnp.bfloat16)
    out_shape = [f32, f32, f32] if mixer == "conv" else [f32, bf16, bf16, v_ext]
    return pl.pallas_call(
        functools.partial(_pre_mixer_kernel, mixer=mixer, rope=rope, alpha=alpha, q_scale=q_scale),
        grid=(nb, s // tm),
        in_specs=in_specs,
        out_specs=[_tok_spec(tm, o.shape[2]) for o in out_shape],
        out_shape=out_shape,
        compiler_params=_cparams(2),
        name="pre_mixer_" + mixer,
    )(*args)


def _stack_heads(q, head0):
    zero = jnp.zeros_like(q)
    return jnp.concatenate([jnp.where(head0, q, zero), jnp.where(head0, zero, q)], axis=0)


def _unstack_heads(o, head0):
    n = o.shape[0] // 2
    return jnp.where(head0, o[:n], o[n:])


def _dot_nt(a, b):
    return lax.dot_general(a, b, (((1,), (1,)), ((), ())), preferred_element_type=jnp.float32)


def _na_kernel(q_ref, k_ref, v_ref, kc_ref, vc_ref, bias_ref, *rest, rows, ctx_queries):
    if ctx_queries:
        qc_ref, o_ref, oc_ref, s_scr, p_scr = rest
    else:
        o_ref, s_scr, p_scr = rest
    head0 = lax.broadcasted_iota(jnp.int32, (1, LANES), 1) < (LANES // HEADS_PER_SLAB)
    nk = NA_KH * GRID_W
    n_stack = HEADS_PER_SLAB * GRID_W
    n_batch, n_ctx = kc_ref.shape[0], kc_ref.shape[1]

    def window_start(r):
        rs = min(max(r - NA_KH // 2, 0), rows - NA_KH)
        return rs, rs * GRID_W

    n_blk = rows // NA_ROW_BLOCK

    def scores(g):
        bb, r0, slot = g // n_blk, (g % n_blk) * NA_ROW_BLOCK, g % NA_RING
        q_blk = q_ref[bb, r0 * GRID_W:(r0 + NA_ROW_BLOCK) * GRID_W, :]
        zero = jnp.zeros_like(q_blk)
        q_heads = (jnp.where(head0, q_blk, zero), jnp.where(head0, zero, q_blk))
        for hh, q_head in enumerate(q_heads):
            s_ctx = _dot_nt(q_head, kc_ref[bb]).reshape(NA_ROW_BLOCK, GRID_W, n_ctx)
            s_scr[slot, :, hh * GRID_W:(hh + 1) * GRID_W, nk:] = s_ctx
        for i in range(NA_ROW_BLOCK):
            r = r0 + i
            rs, k0 = window_start(r)
            qs = jnp.concatenate([q_head[i * GRID_W:(i + 1) * GRID_W] for q_head in q_heads], axis=0)
            dy0 = (NA_KH - 1) - (r - rs)
            bias = jnp.concatenate([bias_ref[dy0 + 2 * m] for m in range(NA_KH // 2)], axis=1)
            s_scr[slot, i, :, :nk] = _dot_nt(qs, k_ref[bb, k0:k0 + nk, :]) + bias

    def softmax(g):
        slot = g % NA_RING
        for i in range(NA_ROW_BLOCK):
            for g0 in range(0, n_stack, NA_SOFTMAX_ROWS):
                s = s_scr[slot, i, g0:g0 + NA_SOFTMAX_ROWS, :]
                z = (s - jnp.max(s, axis=-1, keepdims=True)).astype(p_scr.dtype)
                p_scr[slot, i, g0:g0 + NA_SOFTMAX_ROWS, :] = jnp.exp(z)

    def weighted_values(g):
        bb, r0, slot = g // n_blk, (g % n_blk) * NA_ROW_BLOCK, g % NA_RING
        p_ctx = p_scr[slot, :, :, nk:].reshape(NA_ROW_BLOCK * n_stack, n_ctx)
        o_ctx = jnp.dot(p_ctx, vc_ref[bb], preferred_element_type=jnp.float32)
        for i in range(NA_ROW_BLOCK):
            r = r0 + i
            _, k0 = window_start(r)
            o = (jnp.dot(p_scr[slot, i, :, :nk], v_ref[bb, k0:k0 + nk, :], preferred_element_type=jnp.float32)
                 + o_ctx[i * n_stack:(i + 1) * n_stack])
            o = _unstack_heads(o[:, :LANES] / o[:, LANES:], head0)
            o_ref[bb, r * GRID_W:(r + 1) * GRID_W, :] = o.astype(o_ref.dtype)

    n_all = n_batch * n_blk
    for step in range(n_all + 2):
        if step >= 2:
            weighted_values(step - 2)
        if 1 <= step <= n_all:
            softmax(step - 1)
        if step < n_all:
            scores(step)

    if ctx_queries:
        for bb in range(n_batch):
            s = _dot_nt(_stack_heads(qc_ref[bb], head0), kc_ref[bb])
            p = jnp.exp((s - jnp.max(s, axis=-1, keepdims=True)).astype(jnp.bfloat16))
            o = jnp.dot(p, vc_ref[bb], preferred_element_type=jnp.float32)
            oc_ref[bb] = _unstack_heads(o[:, :LANES] / o[:, LANES:], head0).astype(oc_ref.dtype)


def _na_bias_table(rpb):
    h, n_dy, n_dx = rpb.shape
    n_slab = h // HEADS_PER_SLAB
    col = np.arange(GRID_W)
    win_c0 = np.clip(col - NA_KW // 2, 0, GRID_W - NA_KW)
    allowed = (col[None, :] >= win_c0[:, None]) & (col[None, :] < win_c0[:, None] + NA_KW)
    row, lane = np.arange(HEADS_PER_SLAB * GRID_W), np.arange(2 * GRID_W)
    c_hh, c_e, c_j = np.unravel_index(np.arange(HEADS_PER_SLAB * 2 * n_dx), (HEADS_PER_SLAB, 2, n_dx))
    q, k = row % GRID_W, lane % GRID_W
    ok = allowed[q[:, None], k[None, :]]
    one_hot = ((c_hh[:, None, None] == (row // GRID_W)[None, :, None])
               & (c_e[:, None, None] == (lane // GRID_W)[None, None, :])
               & (c_j[:, None, None] == (k[None, :] - q[:, None] + NA_KW - 1)[None])
               & ok[None])
    one_hot = jnp.asarray(one_hot).astype(rpb.dtype)
    pairs = jnp.stack([rpb[:, :-1], rpb[:, 1:]], axis=2)
    pairs = pairs.reshape(n_slab, HEADS_PER_SLAB, n_dy - 1, 2 * n_dx)
    pairs = jnp.moveaxis(pairs, 1, 2).reshape(n_slab, n_dy - 1, HEADS_PER_SLAB * 2 * n_dx)
    tbl = jnp.einsum("pyc,cab->pyab", pairs, one_hot, precision=lax.Precision.HIGHEST)
    return tbl + jnp.where(ok, 0.0, MASKED).astype(rpb.dtype)


def _neighbourhood_attention(q, k, v, qc, kc, vc, bias, *, ctx_queries):
    nb, s, d = q.shape
    n_ctx = kc.shape[1]
    slabs = d // LANES
    rows = s // GRID_W
    n_stack = HEADS_PER_SLAB * GRID_W
    n_keys = NA_KH * GRID_W + n_ctx
    n_batch = max(n for n in range(1, NA_BATCH + 1) if nb % n == 0)
    lat = pl.BlockSpec((n_batch, s, LANES), lambda p, b: (b, 0, p))
    ctx = pl.BlockSpec((n_batch, n_ctx, LANES), lambda p, b: (b, 0, p))
    lat_v = pl.BlockSpec((n_batch, s, 2 * LANES), lambda p, b: (b, 0, p))
    ctx_v = pl.BlockSpec((n_batch, n_ctx, 2 * LANES), lambda p, b: (b, 0, p))
    in_specs = [lat, lat, lat_v, ctx, ctx_v,
                pl.BlockSpec((None,) + bias.shape[1:], lambda p, b: (p, 0, 0, 0))]
    args = [q, k, v, kc, vc, bias]
    out_specs, out_shape = [lat], [jax.ShapeDtypeStruct(q.shape, jnp.bfloat16)]
    if ctx_queries:
        in_specs.append(ctx)
        args.append(qc)
        out_specs.append(ctx)
        out_shape.append(jax.ShapeDtypeStruct(qc.shape, jnp.bfloat16))
    outs = pl.pallas_call(
        functools.partial(_na_kernel, rows=rows, ctx_queries=ctx_queries),
        grid=(slabs, nb // n_batch),
        in_specs=in_specs, out_specs=out_specs, out_shape=out_shape,
        scratch_shapes=[pltpu.VMEM((NA_RING, NA_ROW_BLOCK, n_stack, n_keys), jnp.float32),
                        pltpu.VMEM((NA_RING, NA_ROW_BLOCK, n_stack, n_keys), jnp.bfloat16)],
        compiler_params=_cparams(2),
        name="neighbourhood_attention",
    )(*args)
    return (outs[0], outs[1]) if ctx_queries else (outs[0], None)


def _post_mixer_kernel(x_ref, *rest, mixer, alpha, seq_len):
    if mixer == "conv":
        bg_ref, u_ref, up_ref, un_ref, cw_ref, *rest = rest
    else:
        o_ref, *rest = rest
    mod_ref, wmix_ref, g1_ref, b1_ref, win_ref, wout_ref, g2_ref, b2_ref, out_ref = rest
    if mixer == "conv":
        t = pl.program_id(1)
        u = u_ref[...]
        tm = u.shape[0]
        row = lax.broadcasted_iota(jnp.int32, (tm, 1), 0)
        if seq_len >= tm:
            per_seq = seq_len // tm
            first = (row == 0) & (t % per_seq == 0)
            last = (row == tm - 1) & (t % per_seq == per_seq - 1)
        else:
            first = functools.reduce(jnp.logical_or, [row == r for r in range(0, tm, seq_len)])
            last = functools.reduce(jnp.logical_or, [row == r + seq_len - 1 for r in range(0, tm, seq_len)])
        u_prev = jnp.where(row == 0, up_ref[SUBLANES - 1:SUBLANES, :], pltpu.roll(u, 1, 0))
        u_next = jnp.where(row == tm - 1, un_ref[0:1, :], pltpu.roll(u, tm - 1, 0))
        u_prev, u_next = jnp.where(first, 0.0, u_prev), jnp.where(last, 0.0, u_next)
        y = cw_ref[0:1, :] * u_prev + cw_ref[1:2, :] * u + cw_ref[2:3, :] * u_next
        mixed = (bg_ref[...] * y).astype(jnp.bfloat16)
    else:
        mixed = o_ref[...]
    sub = min(TM_LATENT, x_ref.shape[0])
    subs = [slice(s0, s0 + sub) for s0 in range(0, x_ref.shape[0], sub)]
    zs = [jnp.dot(mixed[rs], wmix_ref[...], preferred_element_type=jnp.float32) for rs in subs]
    z2s = []
    for rs, z in zip(subs, zs):
        x2 = _layer_norm(alpha * x_ref[rs, :] + 1.0 * mod_ref[5:6, :] * z, g1_ref[...], b1_ref[...])
        z2s.append(_swiglu_pre_norm(x2, mod_ref, 6, win_ref, wout_ref, alpha))
    for rs, z2 in zip(subs, z2s):
        out_ref[rs, :] = _layer_norm(z2, g2_ref[...], b2_ref[...])


def _post_mixer(xt, mixer_in, mod, ln_g, ln_b, w_mix, ffn_w_in, ffn_w_out, conv_w, *, mixer, layer, j, mod_row, tm,
                alpha, seq_len=None):
    nb, s, d = xt.shape
    f = ffn_w_out.shape[2]
    tok = _tok_spec(tm, d)
    if mixer == "conv":
        bg, u = mixer_in
        per_tile = tm // SUBLANES
        last = s // SUBLANES - 1
        prev_spec = pl.BlockSpec((None, SUBLANES, d), lambda b, t: (b, jnp.maximum(t * per_tile - 1, 0), 0))
        next_spec = pl.BlockSpec((None, SUBLANES, d), lambda b, t: (b, jnp.minimum((t + 1) * per_tile, last), 0))
        mix_specs = [tok, tok, prev_spec, next_spec, pl.BlockSpec((None, CONV_W, d), lambda b, t: (j, 0, 0))]
        mix_args = [bg, u, u, u, conv_w]
    else:
        mix_specs, mix_args = [tok], [mixer_in]
    return pl.pallas_call(
        functools.partial(_post_mixer_kernel, mixer=mixer, alpha=alpha, seq_len=seq_len or s),
        grid=(nb, s // tm),
        in_specs=([tok] + mix_specs
                  + [_mod_spec(layer, mod_row, d), _resident((None, d, d), lambda b, t: (j, 0, 0)),
                     _ln_spec(layer, 1, d), _ln_spec(layer, 1, d)]
                  + _ffn_specs(layer, 1, d, f) + [_ln_spec(layer, 2, d), _ln_spec(layer, 2, d)]),
        out_specs=tok,
        out_shape=jax.ShapeDtypeStruct(xt.shape, jnp.float32),
        compiler_params=_cparams(2),
        name="post_mixer_" + mixer,
    )(xt, *mix_args, mod, w_mix, ln_g, ln_b, ffn_w_in, ffn_w_out, ln_g, ln_b)


def _rope_tables(n_tok, head_dim):
    t = np.arange(n_tok)
    n_freq = head_dim // 4
    inv_freq = ROPE_BASE ** (-np.arange(n_freq) / n_freq)
    ang_row = (t // GRID_W)[:, None] * inv_freq[None]
    ang_col = (t % GRID_W)[:, None] * inv_freq[None]
    cos = np.concatenate([np.cos(ang_row)] * 2 + [np.cos(ang_col)] * 2, axis=1)
    sin = np.concatenate([-np.sin(ang_row), np.sin(ang_row), -np.sin(ang_col), np.sin(ang_col)], axis=1)
    reps = LANES // head_dim
    return (jnp.asarray(np.tile(cos, (1, reps)), jnp.float32),
            jnp.asarray(np.tile(sin, (1, reps)), jnp.float32))


def kernel(x, c, ctx, c_ctx, ada_w, ada_b, ln_g, ln_b, ffn_w_in, ffn_w_out, na_w_qkv, na_w_out, na_rpb,
           sc_w_in, sc_conv, sc_w_out):
    nb, s, d = x.shape
    n_ctx = ctx.shape[1]
    depth = ada_w.shape[0]
    head_dim = d // N_HEADS
    alpha = (2.0 * depth) ** 0.25
    q_scale = head_dim ** -0.5
    last_na = max(range(0, depth, N_MIXERS))
    rows = s // GRID_W
    assert s % TM_LATENT == 0 and TM_LATENT % GRID_W == 0
    assert rows >= NA_KH and rows % NA_ROW_BLOCK == 0 and rows // NA_ROW_BLOCK >= 3
    assert LANES // head_dim == HEADS_PER_SLAB

    ctx_row = nb
    n_rows = -(-(nb + 1) // SUBLANES) * SUBLANES
    cc = jnp.concatenate([c, c_ctx[None], jnp.zeros((n_rows - nb - 1, d), c.dtype)], axis=0)
    mod = _ada_modulation(cc, ada_w, ada_b)

    bf = jnp.bfloat16
    ffn_w_in, ffn_w_out = ffn_w_in.astype(bf), ffn_w_out.astype(bf)
    na_w_qkv, na_w_out = na_w_qkv.astype(bf), na_w_out.astype(bf)
    sc_w_in, sc_w_out = sc_w_in.astype(bf), sc_w_out.astype(bf)
    ln_g4, ln_b4 = ln_g.reshape(depth, 3, 1, d), ln_b.reshape(depth, 3, 1, d)
    rope_tabs = _rope_tables(s, head_dim)

    lat = dict(mod_row=None, tm=TM_LATENT)
    pair = max(p for p in range(1, TM_LATENT // n_ctx + 1) if nb % p == 0) if n_ctx <= TM_LATENT else 1
    cx = dict(mod_row=ctx_row, tm=pair * n_ctx)

    def paired(a):
        return a.reshape(nb // pair, pair * n_ctx, a.shape[-1])

    def unpaired(a):
        return a.reshape(nb, n_ctx, a.shape[-1])

    for i in range(depth):
        ctx_in = i <= last_na
        ctx_out = i < last_na
        j = i // N_MIXERS
        mixer = "na" if i % N_MIXERS == 0 else "conv"
        w_proj, w_mix = (na_w_qkv, na_w_out) if mixer == "na" else (sc_w_in, sc_w_out)
        pre = functools.partial(_pre_mixer, mod=mod, ln_g=ln_g4, ln_b=ln_b4, ffn_w_in=ffn_w_in,
                                ffn_w_out=ffn_w_out, w_proj=w_proj, mixer=mixer, layer=i, j=j, alpha=alpha,
                                q_scale=q_scale)
        post = functools.partial(_post_mixer, mod=mod, ln_g=ln_g4, ln_b=ln_b4, w_mix=w_mix, ffn_w_in=ffn_w_in,
                                 ffn_w_out=ffn_w_out, conv_w=sc_conv, mixer=mixer, layer=i, j=j, alpha=alpha)

        x, *x_mix = pre(x, rope_tabs=rope_tabs if mixer == "na" else None, **lat)
        if ctx_in:
            ctx, *c_mix = [unpaired(a) for a in pre(paired(ctx), rope_tabs=None, **cx)]

        if mixer == "na":
            (q, k, v), (qc, kc, vc) = x_mix, c_mix
            x_mix, c_mix = _neighbourhood_attention(q, k, v, qc, kc, vc, _na_bias_table(na_rpb[j]),
                                                    ctx_queries=ctx_out)
            c_mix = paired(c_mix) if ctx_out else None
        else:
            c_mix = [paired(a) for a in c_mix] if ctx_out else None
        x = post(x, x_mix, mod_row=None, tm=TM_POST_NA if mixer == "na" else TM_LATENT)
        if ctx_out:
            ctx = unpaired(post(paired(ctx), c_mix, seq_len=n_ctx, **cx))
    return x
```
